```python
import math
import jax, jax.numpy as jnp
from jax import lax
import numpy as np

D_MODEL = 1024
BATCH = 4
SEQ = 4096
DEPTH = 4

N_MIXERS = 2
N_A_LAYERS = (DEPTH + 1) // 2
N_B_LAYERS = DEPTH // 2
GMLP_WIDTH = D_MODEL
GMLP_CHUNK = 128
GMLP_GROUPS = 8
GMLP_GROUP_DIM = GMLP_WIDTH // GMLP_GROUPS
N_HEADS = 16
HEAD_DIM = D_MODEL // N_HEADS
MOBA_BLOCK = 256
MOBA_TOPK = 3
Q_CHUNK = 128
REL_BUCKETS = 32
REL_MAX_DIST = 128
FFN_DIM = 2816
CONV_WIDTH = 3
LN_EPS = 1e-5
DN_ALPHA = (2 * DEPTH) ** 0.25
DN_BETA = (8 * DEPTH) ** -0.25

kernel_name = "hybrid_gmlp_moba_convffn_deepnorm"


def layer_norm(x, g, b):
    xf = x.astype(jnp.float32)
    mu = jnp.mean(xf, axis=-1, keepdims=True)
    var = jnp.mean(jnp.square(xf - mu), axis=-1, keepdims=True)
    y = (xf - mu) * lax.rsqrt(var + LN_EPS)
    return (y * g.astype(jnp.float32) + b.astype(jnp.float32)).astype(x.dtype)


def rel_bucket(dist):
    n = jnp.maximum(dist, 0)
    max_exact = REL_BUCKETS // 2
    nf = jnp.maximum(n, 1).astype(jnp.float32)
    large = max_exact + (jnp.log(nf / max_exact) / math.log(REL_MAX_DIST / max_exact)
                         * (REL_BUCKETS - max_exact)).astype(jnp.int32)
    large = jnp.minimum(large, REL_BUCKETS - 1)
    return jnp.where(n < max_exact, n, large)


def chunked_gmlp(x, w_in, ln_g, ln_b, w_s, b_s, w_out):
    B, S, _ = x.shape
    z = jax.nn.gelu(x @ w_in)
    u, v = jnp.split(z, 2, axis=-1)
    v = layer_norm(v, ln_g, ln_b)
    nc = S // GMLP_CHUNK
    v = v.reshape(B, nc, GMLP_CHUNK, GMLP_GROUPS, GMLP_GROUP_DIM)
    causal = jnp.tril(jnp.ones((GMLP_CHUNK, GMLP_CHUNK), dtype=bool))
    w = jnp.where(causal[None], w_s, jnp.zeros_like(w_s))
    sv = jnp.einsum('gts,bnsgc->bntgc', w, v) + b_s.T[None, None, :, :, None]
    return (u * sv.reshape(B, S, GMLP_WIDTH)) @ w_out


def moba_attention(x, w_qkv, w_o, rel_bias):
    B, S, _ = x.shape
    H, Dh, L = N_HEADS, HEAD_DIM, MOBA_BLOCK
    nb = -(-S // L)
    sp = nb * L
    nq = S // Q_CHUNK
    topk = min(MOBA_TOPK, nb)
    qkv = (x @ w_qkv).reshape(B, S, 3, H, Dh)
    q = jnp.transpose(qkv[:, :, 0], (0, 2, 1, 3)) * (Dh ** -0.5)
    pad = ((0, 0), (0, 0), (0, sp - S), (0, 0))
    k = jnp.pad(jnp.transpose(qkv[:, :, 1], (0, 2, 1, 3)), pad)
    v = jnp.pad(jnp.transpose(qkv[:, :, 2], (0, 2, 1, 3)), pad)
    kb = k.reshape(B, H, nb, L, Dh)
    vb = v.reshape(B, H, nb, L, Dh)
    kbar = jnp.mean(kb.astype(jnp.float32), axis=3).astype(k.dtype)
    bias_hb = rel_bias.T
    qc = jnp.moveaxis(q.reshape(B, H, nq, Q_CHUNK, Dh), 2, 0)
    bi = jnp.arange(B)[:, None, None, None]
    hi = jnp.arange(H)[None, :, None, None]
    hi5 = hi[..., None]
    blk = jnp.arange(nb)
    off = jnp.arange(L)

    def chunk_fn(args):
        c, q_c = args
        t = c * Q_CHUNK + jnp.arange(Q_CHUNK)
        cur = (c * Q_CHUNK) // L
        gate = jnp.einsum('bhqd,bhnd->bhqn', q_c, kbar).astype(jnp.float32)
        gate = jnp.where(blk < cur, gate, -jnp.inf)
        _, idx = lax.top_k(gate, topk)
        sel_ok = idx < cur
        k_sel = kb[bi, hi, idx]
        v_sel = vb[bi, hi, idx]
        k_own = lax.dynamic_index_in_dim(kb, cur, axis=2, keepdims=False)
        v_own = lax.dynamic_index_in_dim(vb, cur, axis=2, keepdims=False)
        s_sel = jnp.einsum('bhqd,bhqjld->bhqjl', q_c, k_sel).astype(jnp.float32)
        s_own = jnp.einsum('bhqd,bhld->bhql', q_c, k_own).astype(jnp.float32)
        d_sel = t[None, None, :, None, None] - (idx[..., None] * L + off)
        d_own = t[:, None] - (cur * L + off)[None, :]
        s_sel = s_sel + bias_hb[hi5, rel_bucket(d_sel)].astype(jnp.float32)
        s_own = s_own + bias_hb[:, rel_bucket(d_own)][None].astype(jnp.float32)
        s_sel = jnp.where(sel_ok[..., None], s_sel, -jnp.inf)
        s_own = jnp.where(d_own >= 0, s_own, -jnp.inf)
        logits = jnp.concatenate([s_sel.reshape(B, H, Q_CHUNK, topk * L), s_own], axis=-1)
        p = jax.nn.softmax(logits, axis=-1)
        p_sel = p[..., :topk * L].reshape(B, H, Q_CHUNK, topk, L).astype(v.dtype)
        p_own = p[..., topk * L:].astype(v.dtype)
        return (jnp.einsum('bhqjl,bhqjld->bhqd', p_sel, v_sel)
                + jnp.einsum('bhql,bhld->bhqd', p_own, v_own))

    o = lax.map(chunk_fn, (jnp.arange(nq, dtype=jnp.int32), qc))
    o = jnp.transpose(o, (1, 0, 3, 2, 4)).reshape(B, S, H * Dh)
    return o @ w_o


def conv_ffn(x, w_up, conv_w, conv_b, w_down):
    S = x.shape[1]
    h = x @ w_up
    hp = jnp.pad(h, ((0, 0), (CONV_WIDTH - 1, 0), (0, 0)))
    hc = conv_b
    for j in range(CONV_WIDTH):
        hc = hc + conv_w[j] * hp[:, j:j + S]
    g, val = jnp.split(hc, 2, axis=-1)
    return (jax.nn.gelu(g) * val) @ w_down


def setup_inputs(seed: int = 0) -> dict:
    key = jax.random.key(seed)
    ks = jax.random.split(key, 20)
    f32 = jnp.float32
    D = D_MODEL

    def nrm(k, shape, scale):
        return jax.random.normal(k, shape, f32) * scale

    return {
        "x": nrm(ks[0], (BATCH, SEQ, D), 1.0),
        "ln_mix_g": 1.0 + nrm(ks[1], (DEPTH, D), 0.01),
        "ln_mix_b": nrm(ks[2], (DEPTH, D), 0.01),
        "ln_ffn_g": 1.0 + nrm(ks[3], (DEPTH, D), 0.01),
        "ln_ffn_b": nrm(ks[4], (DEPTH, D), 0.01),
        "a_w_in": nrm(ks[5], (N_A_LAYERS, D, 2 * GMLP_WIDTH), D ** -0.5),
        "a_ln_g": 1.0 + nrm(ks[6], (N_A_LAYERS, GMLP_WIDTH), 0.01),
        "a_ln_b": nrm(ks[7], (N_A_LAYERS, GMLP_WIDTH), 0.01),
        "a_w_s": nrm(ks[8], (N_A_LAYERS, GMLP_GROUPS, GMLP_CHUNK, GMLP_CHUNK), GMLP_CHUNK ** -0.5),
        "a_b_s": 1.0 + nrm(ks[9], (N_A_LAYERS, GMLP_GROUPS, GMLP_CHUNK), 0.01),
        "a_w_out": nrm(ks[10], (N_A_LAYERS, GMLP_WIDTH, D), GMLP_WIDTH ** -0.5 * DN_BETA),
        "b_w_qkv": nrm(ks[11], (N_B_LAYERS, D, 3 * D), D ** -0.5),
        "b_w_o": nrm(ks[12], (N_B_LAYERS, D, D), D ** -0.5 * DN_BETA),
        "rel_bias": nrm(ks[13], (REL_BUCKETS, N_HEADS), 0.5),
        "f_w_up": nrm(ks[14], (DEPTH, D, 2 * FFN_DIM), D ** -0.5),
        "f_conv_w": nrm(ks[15], (DEPTH, CONV_WIDTH, 2 * FFN_DIM), CONV_WIDTH ** -0.5),
        "f_conv_b": nrm(ks[16], (DEPTH, 2 * FFN_DIM), 0.01),
        "f_w_down": nrm(ks[17], (DEPTH, FFN_DIM, D), FFN_DIM ** -0.5 * DN_BETA),
    }


def reference(x, ln_mix_g, ln_mix_b, ln_ffn_g, ln_ffn_b, a_w_in, a_ln_g, a_ln_b, a_w_s,
              a_b_s, a_w_out, b_w_qkv, b_w_o, rel_bias, f_w_up, f_conv_w, f_conv_b,
              f_w_down):
    for i in range(DEPTH):
        j = i // N_MIXERS
        if i % N_MIXERS == 0:
            y = chunked_gmlp(x, a_w_in[j], a_ln_g[j], a_ln_b[j], a_w_s[j], a_b_s[j], a_w_out[j])
        else:
            y = moba_attention(x, b_w_qkv[j], b_w_o[j], rel_bias)
        x = layer_norm(DN_ALPHA * x + y, ln_mix_g[i], ln_mix_b[i])
        y = conv_ffn(x, f_w_up[i], f_conv_w[i], f_conv_b[i], f_w_down[i])
        x = layer_norm(DN_ALPHA * x + y, ln_ffn_g[i], ln_ffn_b[i])
    return x
```

```python
import functools
import math

import jax
import jax.numpy as jnp
from jax import lax
from jax.experimental import pallas as pl
from jax.experimental.pallas import tpu as pltpu

F32 = jnp.float32
BF16 = jnp.bfloat16

GMLP_CHUNK = 128
N_HEADS = 16
MOBA_BLOCK = 256
MOBA_TOPK = 3
REL_BUCKETS = 32
REL_MAX_DIST = 128
LN_EPS = 1e-5

LANES_V7X = 128
SUBLANES_V7X = 8
VMEM_LIMIT_BYTES_V7X = 56 * 1024 * 1024

MASK_NEG = -1e30


def _layer_norm(x, g, b):
    mu = jnp.mean(x, axis=-1, keepdims=True)
    xc = x - mu
    var = jnp.mean(xc * xc, axis=-1, keepdims=True)
    return xc * lax.rsqrt(var + LN_EPS) * g + b


def _resident(shape):
    return pl.BlockSpec(shape, lambda *_: (0,) * len(shape), pipeline_mode=pl.Buffered(1))


def _params(n_axes):
    return pltpu.CompilerParams(
        dimension_semantics=("arbitrary",) * n_axes,
        vmem_limit_bytes=VMEM_LIMIT_BYTES_V7X,
    )


def _gmlp_kernel(x_ref, w_in_ref, lng_ref, lnb_ref, ws_ref, bs_ref, w_out_ref,
                 g_ref, b_ref, o_ref, *, alpha, n_groups):
    x = x_ref[...]
    tm = x.shape[0]
    width = w_out_ref.shape[0]
    gdim = width // n_groups
    z = jnp.dot(x.astype(BF16), w_in_ref[...], preferred_element_type=F32)
    z = jax.nn.gelu(z)
    u = z[:, :width]
    v = _layer_norm(z[:, width:], lng_ref[...], lnb_ref[...]).astype(BF16)
    row = lax.broadcasted_iota(jnp.int32, (GMLP_CHUNK, GMLP_CHUNK), 0)
    col = lax.broadcasted_iota(jnp.int32, (GMLP_CHUNK, GMLP_CHUNK), 1)
    causal = col <= row
    w_masked = [jnp.where(causal, ws_ref[g], 0.0).astype(BF16) for g in range(n_groups)]
    chunks = []
    for c in range(tm // GMLP_CHUNK):
        rows = slice(c * GMLP_CHUNK, (c + 1) * GMLP_CHUNK)
        groups = [
            jnp.dot(w_masked[g], v[rows, g * gdim:(g + 1) * gdim], preferred_element_type=F32)
            for g in range(n_groups)
        ]
        chunks.append(jnp.concatenate(groups, axis=1) + bs_ref[...])
    sv = jnp.concatenate(chunks, axis=0)
    y = jnp.dot((u * sv).astype(BF16), w_out_ref[...], preferred_element_type=F32)
    o_ref[...] = _layer_norm(alpha * x + y, g_ref[...], b_ref[...])


def _gmlp_layer(x, w_in, ln_g, ln_b, w_s, b_s, w_out, g, b, *, alpha, tm):
    t, d = x.shape
    width = w_out.shape[0]
    n_groups = w_s.shape[0]
    bs_tile = jnp.repeat(b_s.T, width // n_groups, axis=1)
    row_spec = pl.BlockSpec((tm, d), lambda i: (i, 0))
    return pl.pallas_call(
        functools.partial(_gmlp_kernel, alpha=alpha, n_groups=n_groups),
        grid=(t // tm,),
        in_specs=[
            row_spec,
            _resident(w_in.shape),
            _resident((1, width)), _resident((1, width)),
            _resident(w_s.shape),
            _resident(bs_tile.shape),
            _resident(w_out.shape),
            _resident((1, d)), _resident((1, d)),
        ],
        out_specs=row_spec,
        out_shape=jax.ShapeDtypeStruct((t, d), F32),
        compiler_params=_params(1),
        name="gmlp_layer",
    )(x, w_in.astype(BF16), ln_g[None], ln_b[None], w_s, bs_tile, w_out.astype(BF16),
      g[None], b[None])


def _causal_conv(h, prev, cw, cb):
    def taps(cur, back1, back2):
        return cb + cw[0:1] * back2 + cw[1:2] * back1 + cw[2:3] * cur

    body = taps(h, pltpu.roll(h, 1, 0), pltpu.roll(h, 2, 0))
    head = jnp.concatenate([prev, h[:SUBLANES_V7X]], axis=0)
    top = taps(head, pltpu.roll(head, 1, 0), pltpu.roll(head, 2, 0))[SUBLANES_V7X:]
    return jnp.concatenate([top, body[SUBLANES_V7X:]], axis=0)


def _ffn_kernel(x_ref, w_up_ref, cw_ref, cb_ref, w_down_ref, g_ref, b_ref, o_ref,
                act_ref, carry_ref, *, alpha, fc, tiles_per_seq):
    x = x_ref[...]
    tm = x.shape[0]
    ffn = w_down_ref.shape[0]

    @pl.when(pl.program_id(0) % tiles_per_seq == 0)
    def _():
        carry_ref[...] = jnp.zeros_like(carry_ref)

    xb = x.astype(BF16)

    def conv_cols(start):
        cols = slice(start, start + fc)
        h = jnp.dot(xb, w_up_ref[:, cols], preferred_element_type=F32)
        prev = carry_ref[:, cols]
        carry_ref[:, cols] = h[tm - SUBLANES_V7X:]
        return _causal_conv(h, prev, cw_ref[:, cols], cb_ref[:, cols])

    for j in range(ffn // fc):
        gate = conv_cols(j * fc)
        val = conv_cols(ffn + j * fc)
        act_ref[:, j * fc:(j + 1) * fc] = (jax.nn.gelu(gate) * val).astype(BF16)

    y = jnp.dot(act_ref[...], w_down_ref[...], preferred_element_type=F32)
    o_ref[...] = _layer_norm(alpha * x + y, g_ref[...], b_ref[...])


def _ffn_layer(x, w_up, conv_w, conv_b, w_down, g, b, *, alpha, tm, fc, seq):
    t, d = x.shape
    ffn = w_down.shape[0]
    row_spec = pl.BlockSpec((tm, d), lambda i: (i, 0))
    return pl.pallas_call(
        functools.partial(_ffn_kernel, alpha=alpha, fc=fc, tiles_per_seq=seq // tm),
        grid=(t // tm,),
        in_specs=[
            row_spec,
            _resident(w_up.shape),
            _resident(conv_w.shape),
            _resident((1, 2 * ffn)),
            _resident(w_down.shape),
            _resident((1, d)), _resident((1, d)),
        ],
        out_specs=row_spec,
        out_shape=jax.ShapeDtypeStruct((t, d), F32),
        scratch_shapes=[
            pltpu.VMEM((tm, ffn), BF16),
            pltpu.VMEM((SUBLANES_V7X, 2 * ffn), F32),
        ],
        compiler_params=_params(1),
        name="conv_ffn_layer",
    )(x, w_up.astype(BF16), conv_w, conv_b[None], w_down.astype(BF16), g[None], b[None])


def _qkv_kernel(x_ref, wq_ref, wk_ref, wv_ref, q_ref, k_ref, v_ref, kbar_ref,
                *, q_scale, tiles_per_seq):
    xb = x_ref[...].astype(BF16)
    tm = xb.shape[0]
    q = jnp.dot(xb, wq_ref[...], preferred_element_type=F32) * q_scale
    k = jnp.dot(xb, wk_ref[...], preferred_element_type=F32)
    v = jnp.dot(xb, wv_ref[...], preferred_element_type=F32)
    pos = (pl.program_id(0) % tiles_per_seq) * tm + lax.broadcasted_iota(
        jnp.int32, (tm, LANES_V7X), 0)
    lane = lax.broadcasted_iota(jnp.int32, (tm, LANES_V7X), 1)
    block_onehot = jnp.where(pos // MOBA_BLOCK == lane, 1.0, 0.0).astype(BF16)
    n_pairs = q.shape[1] // LANES_V7X
    for p in range(n_pairs):
        cols = slice(p * LANES_V7X, (p + 1) * LANES_V7X)
        q_ref[0, p] = q[:, cols].astype(BF16)
        k_ref[0, p, :, :LANES_V7X] = k[:, cols].astype(BF16)
        k_ref[0, p, :, LANES_V7X:] = block_onehot
        v_ref[0, p] = v[:, cols].astype(BF16)
    kbar_ref[0] = jnp.mean(k.reshape(tm // MOBA_BLOCK, MOBA_BLOCK, k.shape[1]), axis=1)


def _qkv_proj(x, w_qkv, *, batch, seq, tm, tn):
    t, d = x.shape
    n_pairs = d // LANES_V7X
    pairs_per_step = tn // LANES_V7X
    tiles_per_seq = seq // tm
    n_col = d // tn
    head_dim = d // N_HEADS

    def out_map(i, j):
        return (i // tiles_per_seq, j, i % tiles_per_seq, 0)

    return pl.pallas_call(
        functools.partial(_qkv_kernel, q_scale=head_dim ** -0.5, tiles_per_seq=tiles_per_seq),
        grid=(t // tm, n_col),
        in_specs=[
            pl.BlockSpec((tm, d), lambda i, j: (i, 0)),
            pl.BlockSpec((d, tn), lambda i, j: (0, j)),
            pl.BlockSpec((d, tn), lambda i, j: (0, n_col + j)),
            pl.BlockSpec((d, tn), lambda i, j: (0, 2 * n_col + j)),
        ],
        out_specs=[
            pl.BlockSpec((1, pairs_per_step, tm, LANES_V7X), out_map),
            pl.BlockSpec((1, pairs_per_step, tm, 2 * LANES_V7X), out_map),
            pl.BlockSpec((1, pairs_per_step, tm, LANES_V7X), out_map),
            pl.BlockSpec((1, tm // MOBA_BLOCK, tn),
                         lambda i, j: (i // tiles_per_seq, i % tiles_per_seq, j)),
        ],
        out_shape=[
            jax.ShapeDtypeStruct((batch, n_pairs, seq, LANES_V7X), BF16),
            jax.ShapeDtypeStruct((batch, n_pairs, seq, 2 * LANES_V7X), BF16),
            jax.ShapeDtypeStruct((batch, n_pairs, seq, LANES_V7X), BF16),
            jax.ShapeDtypeStruct((batch, seq // MOBA_BLOCK, d), F32),
        ],
        compiler_params=_params(2),
        name="moba_qkv_proj",
    )(x, w_qkv, w_qkv, w_qkv)


def _moba_attn_kernel(q_ref, k_ref, v_ref, kbar_ref, tab_ref, o_ref, *, head_dim):
    cur = pl.program_id(2)
    q2 = q_ref[0, 0]
    blk_len = q2.shape[0]
    n_blocks = kbar_ref.shape[1]
    lane = lax.broadcasted_iota(jnp.int32, q2.shape, 1)
    nt_dims = (((1,), (1,)), ((), ()))

    def scores(qaug, n):
        start = pl.multiple_of(n * blk_len, blk_len)
        return lax.dot_general(qaug, k_ref[0, 0, pl.ds(start, blk_len), :], nt_dims,
                               preferred_element_type=F32)

    def weighted_values(p, n):
        start = pl.multiple_of(n * blk_len, blk_len)
        return jnp.dot(p.astype(BF16), v_ref[0, 0, pl.ds(start, blk_len), :],
                       preferred_element_type=F32)

    heads = []
    for h in range(2):
        in_head = (lane >= head_dim) if h else (lane < head_dim)
        qm = jnp.where(in_head, q2, jnp.zeros_like(q2))
        gate = lax.dot_general(kbar_ref[0].astype(BF16), qm, nt_dims,
                               preferred_element_type=F32)
        blk = lax.broadcasted_iota(jnp.int32, gate.shape, 0)
        gate = jnp.where(blk < cur, gate, -jnp.inf)
        rank = jnp.zeros(gate.shape, F32)
        for m in range(n_blocks):
            gm = gate[m:m + 1, :]
            tie = jnp.where(blk > m, 1.0, 0.0)
            rank = rank + jnp.where(gm > gate, 1.0, jnp.where(gm == gate, tie, 0.0))
        feat_t = jnp.where(blk < cur,
                           jnp.where(rank < MOBA_TOPK, 0.0, MASK_NEG),
                           jnp.where(blk == cur, 0.0, MASK_NEG))
        feat_t = jnp.concatenate(
            [feat_t, jnp.zeros((LANES_V7X - n_blocks, blk_len), F32)], axis=0)
        qaug = jnp.concatenate([qm, feat_t.T.astype(BF16)], axis=1)

        s = scores(qaug, cur) + tab_ref[h, 0]
        m0 = jnp.max(s, axis=1, keepdims=True)
        p = jnp.exp(s - m0)
        carry = (m0, jnp.sum(p, axis=1, keepdims=True), weighted_values(p, cur))

        def step(n, carry, table):
            m_i, l_i, acc = carry
            s = scores(qaug, n)
            if table is not None:
                s = s + table
            m_new = jnp.maximum(m_i, jnp.max(s, axis=1, keepdims=True))
            a = jnp.exp(m_i - m_new)
            p = jnp.exp(s - m_new)
            l_new = a * l_i + jnp.sum(p, axis=1, keepdims=True)
            return m_new, l_new, a * acc + weighted_values(p, n)

        carry = lax.fori_loop(
            0, jnp.minimum(cur, 1),
            lambda _, c: step(cur - 1, c, tab_ref[h, 1]), carry)
        carry = lax.fori_loop(
            0, jnp.maximum(cur - 1, 0),
            lambda n, c: step(n, c, None), carry)
        _, l_f, acc = carry
        heads.append(acc / l_f)
    o_ref[0] = jnp.where(lane < head_dim, heads[0], heads[1]).astype(o_ref.dtype)


def _rel_bucket(dist):
    n = jnp.maximum(dist, 0)
    max_exact = REL_BUCKETS // 2
    nf = jnp.maximum(n, 1).astype(F32)
    large = max_exact + (jnp.log(nf / max_exact) / math.log(REL_MAX_DIST / max_exact)
                         * (REL_BUCKETS - max_exact)).astype(jnp.int32)
    large = jnp.minimum(large, REL_BUCKETS - 1)
    return jnp.where(n < max_exact, n, large)


def _bias_tables(rel_bias):
    qi = jnp.arange(MOBA_BLOCK)[:, None]
    kj = jnp.arange(MOBA_BLOCK)[None, :]
    bias_h = rel_bias.T
    far = bias_h[:, REL_BUCKETS - 1][:, None, None]
    own = bias_h[:, _rel_bucket(qi - kj)] - far
    own = jnp.where((kj <= qi)[None], own, MASK_NEG)
    prev = bias_h[:, _rel_bucket(MOBA_BLOCK + qi - kj)] - far
    return jnp.stack([own, prev], axis=1)


def _moba_attention(q2, kaug, v2, kbar, tables, *, batch, seq):
    n_pairs = q2.shape[1]
    d = n_pairs * LANES_V7X
    n_blocks = seq // MOBA_BLOCK
    return pl.pallas_call(
        functools.partial(_moba_attn_kernel, head_dim=d // N_HEADS),
        grid=(n_pairs, batch, n_blocks),
        in_specs=[
            pl.BlockSpec((1, 1, MOBA_BLOCK, LANES_V7X), lambda p, b, t: (b, p, t, 0)),
            pl.BlockSpec((1, 1, seq, 2 * LANES_V7X), lambda p, b, t: (b, p, 0, 0)),
            pl.BlockSpec((1, 1, seq, LANES_V7X), lambda p, b, t: (b, p, 0, 0)),
            pl.BlockSpec((1, n_blocks, LANES_V7X), lambda p, b, t: (b, 0, p)),
            pl.BlockSpec((2, 2, MOBA_BLOCK, MOBA_BLOCK), lambda p, b, t: (p, 0, 0, 0)),
        ],
        out_specs=pl.BlockSpec((1, MOBA_BLOCK, LANES_V7X), lambda p, b, t: (b, t, p)),
        out_shape=jax.ShapeDtypeStruct((batch, seq, d), BF16),
        compiler_params=_params(3),
        name="moba_attention",
    )(q2, kaug, v2, kbar, tables)


def _out_proj_kernel(a_ref, w_ref, x_ref, g_ref, b_ref, o_ref, *, alpha):
    y = jnp.dot(a_ref[...], w_ref[...], preferred_element_type=F32)
    o_ref[...] = _layer_norm(alpha * x_ref[...] + y, g_ref[...], b_ref[...])


def _out_proj_layer(a, w_o, x, g, b, *, alpha, tm):
    t, d = x.shape
    row_spec = pl.BlockSpec((tm, d), lambda i: (i, 0))
    return pl.pallas_call(
        functools.partial(_out_proj_kernel, alpha=alpha),
        grid=(t // tm,),
        in_specs=[row_spec, _resident(w_o.shape), row_spec, _resident((1, d)), _resident((1, d))],
        out_specs=row_spec,
        out_shape=jax.ShapeDtypeStruct((t, d), F32),
        compiler_params=_params(1),
        name="moba_out_proj",
    )(a, w_o.astype(BF16), x, g[None], b[None])


def _tiles(seq):
    assert seq % (SUBLANES_V7X * MOBA_BLOCK) == 0, "qkv tile must hold 8 whole key blocks"
    return dict(tm_mix=min(seq, 512), tm_ffn=min(seq, 512), tm_qkv=SUBLANES_V7X * MOBA_BLOCK,
                tn_qkv=2 * LANES_V7X, fc_ffn=2 * LANES_V7X)


def kernel(x, ln_mix_g, ln_mix_b, ln_ffn_g, ln_ffn_b, a_w_in, a_ln_g, a_ln_b, a_w_s, a_b_s,
           a_w_out, b_w_qkv, b_w_o, rel_bias, f_w_up, f_conv_w, f_conv_b, f_w_down):
    batch, seq, d = x.shape
    depth = ln_mix_g.shape[0]
    alpha = (2 * depth) ** 0.25
    cfg = _tiles(seq)
    tables = _bias_tables(rel_bias)
    h = x.reshape(batch * seq, d)
    for i in range(depth):
        j = i // 2
        if i % 2 == 0:
            h = _gmlp_layer(h, a_w_in[j], a_ln_g[j], a_ln_b[j], a_w_s[j], a_b_s[j], a_w_out[j],
                            ln_mix_g[i], ln_mix_b[i], alpha=alpha, tm=cfg["tm_mix"])
        else:
            q2, kaug, v2, kbar = _qkv_proj(h, b_w_qkv[j].astype(BF16), batch=batch, seq=seq,
                                           tm=cfg["tm_qkv"], tn=cfg["tn_qkv"])
            attn = _moba_attention(q2, kaug, v2, kbar, tables, batch=batch, seq=seq)
            h = _out_proj_layer(attn.reshape(batch * seq, d), b_w_o[j], h,
                                ln_mix_g[i], ln_mix_b[i], alpha=alpha, tm=cfg["tm_mix"])
        h = _ffn_layer(h, f_w_up[i], f_conv_w[i], f_conv_b[i], f_w_down[i],
                       ln_ffn_g[i], ln_ffn_b[i], alpha=alpha, tm=cfg["tm_ffn"],
                       fc=cfg["fc_ffn"], seq=seq)
    return h.reshape(batch, seq, d)
```

```python
import functools
import math

import jax
import jax.numpy as jnp
from jax import lax
from jax.experimental import pallas as pl
from jax.experimental.pallas import tpu as pltpu

F32 = jnp.float32
BF16 = jnp.bfloat16

GMLP_CHUNK = 128
N_HEADS = 16
MOBA_BLOCK = 256
MOBA_TOPK = 3
REL_BUCKETS = 32
REL_MAX_DIST = 128
LN_EPS = 1e-5

LANES_V7X = 128
SUBLANES_V7X = 8
BF16_SUBLANES_V7X = 16
VMEM_LIMIT_BYTES_V7X = 56 * 1024 * 1024

MASK_NEG = -1e30


def _layer_norm(x, g, b):
    mu = jnp.mean(x, axis=-1, keepdims=True)
    xc = x - mu
    var = jnp.mean(xc * xc, axis=-1, keepdims=True)
    return xc * lax.rsqrt(var + LN_EPS) * g + b


def _resident(shape):
    return pl.BlockSpec(shape, lambda *_: (0,) * len(shape), pipeline_mode=pl.Buffered(1))


def _params(n_axes):
    return pltpu.CompilerParams(
        dimension_semantics=("arbitrary",) * n_axes,
        vmem_limit_bytes=VMEM_LIMIT_BYTES_V7X,
    )


def _gmlp_kernel(x_ref, w_in_ref, lng_ref, lnb_ref, ws_ref, bs_ref, w_out_ref,
                 g_ref, b_ref, o_ref, *, alpha, n_groups):
    x = x_ref[...]
    tm = x.shape[0]
    width = w_out_ref.shape[0]
    gdim = width // n_groups
    z = jnp.dot(x.astype(BF16), w_in_ref[...], preferred_element_type=F32)
    z = jax.nn.gelu(z)
    u = z[:, :width]
    v = _layer_norm(z[:, width:], lng_ref[...], lnb_ref[...]).astype(BF16)
    row = lax.broadcasted_iota(jnp.int32, (GMLP_CHUNK, GMLP_CHUNK), 0)
    col = lax.broadcasted_iota(jnp.int32, (GMLP_CHUNK, GMLP_CHUNK), 1)
    causal = col <= row
    w_masked = [jnp.where(causal, ws_ref[g], 0.0).astype(BF16) for g in range(n_groups)]
    chunks = []
    for c in range(tm // GMLP_CHUNK):
        rows = slice(c * GMLP_CHUNK, (c + 1) * GMLP_CHUNK)
        groups = [
            jnp.dot(w_masked[g], v[rows, g * gdim:(g + 1) * gdim], preferred_element_type=F32)
            for g in range(n_groups)
        ]
        chunks.append(jnp.concatenate(groups, axis=1) + bs_ref[...])
    sv = jnp.concatenate(chunks, axis=0)
    y = jnp.dot((u * sv).astype(BF16), w_out_ref[...], preferred_element_type=F32)
    o_ref[...] = _layer_norm(alpha * x + y, g_ref[...], b_ref[...])


def _gmlp_layer(x, w_in, ln_g, ln_b, w_s, b_s, w_out, g, b, *, alpha, tm):
    t, d = x.shape
    width = w_out.shape[0]
    n_groups = w_s.shape[0]
    bs_tile = jnp.repeat(b_s.T, width // n_groups, axis=1)
    row_spec = pl.BlockSpec((tm, d), lambda i: (i, 0))
    return pl.pallas_call(
        functools.partial(_gmlp_kernel, alpha=alpha, n_groups=n_groups),
        grid=(t // tm,),
        in_specs=[
            row_spec,
            _resident(w_in.shape),
            _resident((1, width)), _resident((1, width)),
            _resident(w_s.shape),
            _resident(bs_tile.shape),
            _resident(w_out.shape),
            _resident((1, d)), _resident((1, d)),
        ],
        out_specs=row_spec,
        out_shape=jax.ShapeDtypeStruct((t, d), F32),
        compiler_params=_params(1),
        name="gmlp_layer",
    )(x, w_in.astype(BF16), ln_g[None], ln_b[None], w_s, bs_tile, w_out.astype(BF16),
      g[None], b[None])


def _causal_conv(h, prev, cw, cb):
    def taps(cur, back1, back2):
        return cb + cw[0:1] * back2 + cw[1:2] * back1 + cw[2:3] * cur

    body = taps(h, pltpu.roll(h, 1, 0), pltpu.roll(h, 2, 0))
    head = jnp.concatenate([prev, h[:SUBLANES_V7X]], axis=0)
    top = taps(head, pltpu.roll(head, 1, 0), pltpu.roll(head, 2, 0))[SUBLANES_V7X:]
    return jnp.concatenate([top, body[SUBLANES_V7X:]], axis=0)


def _ffn_kernel(x_ref, w_up_ref, cw_ref, cb_ref, w_down_ref, g_ref, b_ref, o_ref,
                act_ref, carry_ref, *, alpha, fc, tiles_per_seq):
    x = x_ref[...]
    tm = x.shape[0]
    ffn = w_down_ref.shape[0]

    @pl.when(pl.program_id(0) % tiles_per_seq == 0)
    def _():
        carry_ref[...] = jnp.zeros_like(carry_ref)

    xb = x.astype(BF16)

    def conv_cols(start):
        cols = slice(start, start + fc)
        h = jnp.dot(xb, w_up_ref[:, cols], preferred_element_type=F32)
        prev = carry_ref[:, cols]
        carry_ref[:, cols] = h[tm - SUBLANES_V7X:]
        return _causal_conv(h, prev, cw_ref[:, cols], cb_ref[:, cols])

    for j in range(ffn // fc):
        gate = conv_cols(j * fc)
        val = conv_cols(ffn + j * fc)
        act_ref[:, j * fc:(j + 1) * fc] = (jax.nn.gelu(gate) * val).astype(BF16)

    y = jnp.dot(act_ref[...], w_down_ref[...], preferred_element_type=F32)
    o_ref[...] = _layer_norm(alpha * x + y, g_ref[...], b_ref[...])


def _ffn_layer(x, w_up, conv_w, conv_b, w_down, g, b, *, alpha, tm, fc, seq):
    t, d = x.shape
    ffn = w_down.shape[0]
    row_spec = pl.BlockSpec((tm, d), lambda i: (i, 0))
    return pl.pallas_call(
        functools.partial(_ffn_kernel, alpha=alpha, fc=fc, tiles_per_seq=seq // tm),
        grid=(t // tm,),
        in_specs=[
            row_spec,
            _resident(w_up.shape),
            _resident(conv_w.shape),
            _resident((1, 2 * ffn)),
            _resident(w_down.shape),
            _resident((1, d)), _resident((1, d)),
        ],
        out_specs=row_spec,
        out_shape=jax.ShapeDtypeStruct((t, d), F32),
        scratch_shapes=[
            pltpu.VMEM((tm, ffn), BF16),
            pltpu.VMEM((SUBLANES_V7X, 2 * ffn), F32),
        ],
        compiler_params=_params(1),
        name="conv_ffn_layer",
    )(x, w_up.astype(BF16), conv_w, conv_b[None], w_down.astype(BF16), g[None], b[None])


def _qkv_kernel(x_ref, wq_ref, wk_ref, wv_ref, q_ref, k_ref, v_ref, kbar_ref,
                *, head_dim, tiles_per_seq):
    q_scale = head_dim ** -0.5
    xb = x_ref[...].astype(BF16)
    tm = xb.shape[0]
    q = jnp.dot(xb, wq_ref[...], preferred_element_type=F32) * q_scale
    k = jnp.dot(xb, wk_ref[...], preferred_element_type=F32)
    v = jnp.dot(xb, wv_ref[...], preferred_element_type=F32)
    pos = (pl.program_id(0) % tiles_per_seq) * tm + lax.broadcasted_iota(
        jnp.int32, (tm, LANES_V7X), 0)
    lane = lax.broadcasted_iota(jnp.int32, (tm, LANES_V7X), 1)
    block_onehot = jnp.where(pos // MOBA_BLOCK == lane, 1.0, 0.0).astype(BF16)
    n_pairs = q.shape[1] // LANES_V7X
    q_t = q.T.astype(BF16)
    v_t = v.T.astype(BF16)
    for p in range(n_pairs):
        cols = slice(p * LANES_V7X, (p + 1) * LANES_V7X)
        q_ref[0, p] = q_t[cols]
        k_ref[0, p, :, :LANES_V7X] = k[:, cols].astype(BF16)
        k_ref[0, p, :, LANES_V7X:] = block_onehot
    n_heads = q.shape[1] // head_dim
    for h in range(n_heads):
        v_ref[0, h, :head_dim] = v_t[h * head_dim:(h + 1) * head_dim]
        v_ref[0, h, head_dim:] = jnp.ones((v_ref.shape[2] - head_dim, tm), BF16)
    kbar_ref[0] = jnp.mean(k.reshape(tm // MOBA_BLOCK, MOBA_BLOCK, k.shape[1]), axis=1)


def _qkv_proj(x, w_qkv, *, batch, seq, tm, tn):
    t, d = x.shape
    n_pairs = d // LANES_V7X
    pairs_per_step = tn // LANES_V7X
    tiles_per_seq = seq // tm
    n_col = d // tn
    head_dim = d // N_HEADS
    heads_per_step = tn // head_dim
    v_rows = head_dim + BF16_SUBLANES_V7X

    def row_major_map(i, j):
        return (i // tiles_per_seq, j, i % tiles_per_seq, 0)

    def feature_major_map(i, j):
        return (i // tiles_per_seq, j, 0, i % tiles_per_seq)

    return pl.pallas_call(
        functools.partial(_qkv_kernel, head_dim=head_dim, tiles_per_seq=tiles_per_seq),
        grid=(t // tm, n_col),
        in_specs=[
            pl.BlockSpec((tm, d), lambda i, j: (i, 0)),
            pl.BlockSpec((d, tn), lambda i, j: (0, j)),
            pl.BlockSpec((d, tn), lambda i, j: (0, n_col + j)),
            pl.BlockSpec((d, tn), lambda i, j: (0, 2 * n_col + j)),
        ],
        out_specs=[
            pl.BlockSpec((1, pairs_per_step, LANES_V7X, tm), feature_major_map),
            pl.BlockSpec((1, pairs_per_step, tm, 2 * LANES_V7X), row_major_map),
            pl.BlockSpec((1, heads_per_step, v_rows, tm), feature_major_map),
            pl.BlockSpec((1, tm // MOBA_BLOCK, tn),
                         lambda i, j: (i // tiles_per_seq, i % tiles_per_seq, j)),
        ],
        out_shape=[
            jax.ShapeDtypeStruct((batch, n_pairs, LANES_V7X, seq), BF16),
            jax.ShapeDtypeStruct((batch, n_pairs, seq, 2 * LANES_V7X), BF16),
            jax.ShapeDtypeStruct((batch, N_HEADS, v_rows, seq), BF16),
            jax.ShapeDtypeStruct((batch, seq // MOBA_BLOCK, d), F32),
        ],
        compiler_params=_params(2),
        name="moba_qkv_proj",
    )(x, w_qkv, w_qkv, w_qkv)


def _moba_attn_kernel(q_ref, k_ref, v_ref, kbar_ref, tab_ref, o_ref,
                      s_ref, cmax_ref, acc_ref, *, head_dim):
    cur = pl.program_id(2)
    q_t = q_ref[0, 0]
    blk_len = q_t.shape[1]
    n_blocks = kbar_ref.shape[1]
    feat_row = lax.broadcasted_iota(jnp.int32, q_t.shape, 0)
    kbar = kbar_ref[0].astype(BF16)

    q_aug = []
    for h in range(2):
        in_head = (feat_row >= head_dim) if h else (feat_row < head_dim)
        qm = jnp.where(in_head, q_t, jnp.zeros_like(q_t))
        gate = jnp.dot(kbar, qm, preferred_element_type=F32)
        blk = lax.broadcasted_iota(jnp.int32, gate.shape, 0)
        gate = jnp.where(blk < cur, gate, -jnp.inf)
        rank = jnp.zeros(gate.shape, F32)
        for m in range(n_blocks):
            gm = gate[m:m + 1, :]
            tie = jnp.where(blk > m, 1.0, 0.0)
            rank = rank + jnp.where(gm > gate, 1.0, jnp.where(gm == gate, tie, 0.0))
        feat = jnp.where(blk < cur,
                         jnp.where(rank < MOBA_TOPK, 0.0, MASK_NEG),
                         jnp.where(blk == cur, 0.0, MASK_NEG))
        feat = jnp.concatenate(
            [feat, jnp.zeros((LANES_V7X - n_blocks, blk_len), F32)], axis=0)
        q_aug.append(jnp.concatenate([qm, feat.astype(BF16)], axis=0))
    q_aug = jnp.concatenate(q_aug, axis=1)
    width = q_aug.shape[1]

    def scores(first, count):
        start = pl.multiple_of(first * blk_len, blk_len)
        return jnp.dot(k_ref[0, 0, pl.ds(start, count * blk_len), :], q_aug,
                       preferred_element_type=F32)

    def col_max(s):
        return jnp.max(s.reshape(s.shape[0] // SUBLANES_V7X, SUBLANES_V7X, width), axis=0)

    def keep(first, count, s):
        s_ref[pl.ds(first, count)] = s.reshape(count, blk_len, width)
        return col_max(s)

    no_prev = jnp.where(cur == 0, MASK_NEG, 0.0)
    s_own = scores(cur, 1) + tab_ref[0, 0]
    s_prev = scores(jnp.maximum(cur - 1, 0), 1) + tab_ref[0, 1] + no_prev
    cmax = jnp.maximum(keep(cur, 1, s_own),
                       keep(jnp.where(cur == 0, n_blocks, cur - 1), 1, s_prev))

    n_far = jnp.maximum(cur - 1, 0)
    cmax = lax.fori_loop(
        0, n_far // 2,
        lambda i, c: jnp.maximum(c, keep(2 * i, 2, scores(2 * i, 2))), cmax)
    cmax_ref[...] = cmax

    @pl.when(n_far % 2 == 1)
    def _():
        cmax_ref[...] = jnp.maximum(cmax_ref[...], keep(n_far - 1, 1, scores(n_far - 1, 1)))

    col_m = jnp.max(cmax_ref[...], axis=0, keepdims=True)

    acc_ref[...] = jnp.zeros_like(acc_ref)

    def accumulate(first, count):
        s = s_ref[pl.ds(first, count)].reshape(count * blk_len, width)
        p = jnp.exp(s - col_m).astype(BF16)
        start = pl.multiple_of(first * blk_len, blk_len)
        for h in range(2):
            acc_ref[h] += jnp.dot(v_ref[0, h, :, pl.ds(start, count * blk_len)],
                                  p[:, h * blk_len:(h + 1) * blk_len],
                                  preferred_element_type=F32)

    def accumulate_pair(i, carry):
        accumulate(2 * i, 2)
        return carry

    lax.fori_loop(0, (cur + 1) // 2, accumulate_pair, 0)

    @pl.when(cur % 2 == 0)
    def _():
        accumulate(cur, 1)

    heads = []
    for h in range(2):
        acc = acc_ref[h]
        heads.append(acc[:head_dim] / acc[head_dim:head_dim + 1])
    o_ref[0] = jnp.concatenate(heads, axis=0).T.astype(o_ref.dtype)


def _rel_bucket(dist):
    n = jnp.maximum(dist, 0)
    max_exact = REL_BUCKETS // 2
    nf = jnp.maximum(n, 1).astype(F32)
    large = max_exact + (jnp.log(nf / max_exact) / math.log(REL_MAX_DIST / max_exact)
                         * (REL_BUCKETS - max_exact)).astype(jnp.int32)
    large = jnp.minimum(large, REL_BUCKETS - 1)
    return jnp.where(n < max_exact, n, large)


def _bias_tables(rel_bias):
    blk = MOBA_BLOCK
    n_heads = rel_bias.shape[1]
    bias_h = rel_bias.T
    dist = jnp.arange(-(blk - 1), 2 * blk)
    by_dist = bias_h[:, _rel_bucket(dist)] - bias_h[:, REL_BUCKETS - 1:]
    by_dist = jnp.where(dist[None] < 0, MASK_NEG, by_dist)
    period = 3 * blk
    rows = jnp.pad(by_dist, ((0, 0), (0, 1)))
    skew = jnp.broadcast_to(rows[:, None, :], (n_heads, blk, period))
    skew = skew.reshape(n_heads, blk * period)[:, :blk * (period - 1)]
    toeplitz = skew.reshape(n_heads, blk, period - 1)[:, :, blk - 1:]
    tables = toeplitz.reshape(n_heads // 2, 2, blk, 2, blk)
    return tables.transpose(0, 3, 2, 1, 4).reshape(n_heads // 2, 2, blk, 2 * blk)


def _moba_attention(q_t, kaug, v_t, kbar, tables, *, batch, seq):
    n_pairs = q_t.shape[1]
    d = n_pairs * LANES_V7X
    blk = MOBA_BLOCK
    n_blocks = seq // blk
    v_rows = v_t.shape[2]
    return pl.pallas_call(
        functools.partial(_moba_attn_kernel, head_dim=d // N_HEADS),
        grid=(n_pairs, batch, n_blocks),
        in_specs=[
            pl.BlockSpec((1, 1, LANES_V7X, blk), lambda p, b, t: (b, p, 0, t)),
            pl.BlockSpec((1, 1, seq, 2 * LANES_V7X), lambda p, b, t: (b, p, 0, 0)),
            pl.BlockSpec((1, 2, v_rows, seq), lambda p, b, t: (b, p, 0, 0)),
            pl.BlockSpec((1, n_blocks, LANES_V7X), lambda p, b, t: (b, 0, p)),
            pl.BlockSpec((1, 2, blk, 2 * blk), lambda p, b, t: (p, 0, 0, 0)),
        ],
        out_specs=pl.BlockSpec((1, blk, LANES_V7X), lambda p, b, t: (b, t, p)),
        out_shape=jax.ShapeDtypeStruct((batch, seq, d), BF16),
        scratch_shapes=[
            pltpu.VMEM((n_blocks + 1, blk, 2 * blk), F32),
            pltpu.VMEM((SUBLANES_V7X, 2 * blk), F32),
            pltpu.VMEM((2, v_rows, blk), F32),
        ],
        compiler_params=_params(3),
        name="moba_attention",
    )(q_t, kaug, v_t, kbar, tables)


def _out_proj_kernel(a_ref, w_ref, x_ref, g_ref, b_ref, o_ref, *, alpha):
    y = jnp.dot(a_ref[...], w_ref[...], preferred_element_type=F32)
    o_ref[...] = _layer_norm(alpha * x_ref[...] + y, g_ref[...], b_ref[...])


def _out_proj_layer(a, w_o, x, g, b, *, alpha, tm):
    t, d = x.shape
    row_spec = pl.BlockSpec((tm, d), lambda i: (i, 0))
    return pl.pallas_call(
        functools.partial(_out_proj_kernel, alpha=alpha),
        grid=(t // tm,),
        in_specs=[row_spec, _resident(w_o.shape), row_spec, _resident((1, d)), _resident((1, d))],
        out_specs=row_spec,
        out_shape=jax.ShapeDtypeStruct((t, d), F32),
        compiler_params=_params(1),
        name="moba_out_proj",
    )(a, w_o.astype(BF16), x, g[None], b[None])


def _tiles(seq):
    assert seq % (SUBLANES_V7X * MOBA_BLOCK) == 0, "qkv tile must hold 8 whole key blocks"
    return dict(tm_mix=min(seq, 512), tm_ffn=min(seq, 512), tm_qkv=SUBLANES_V7X * MOBA_BLOCK,
                tn_qkv=2 * LANES_V7X, fc_ffn=2 * LANES_V7X)


def kernel(x, ln_mix_g, ln_mix_b, ln_ffn_g, ln_ffn_b, a_w_in, a_ln_g, a_ln_b, a_w_s, a_b_s,
           a_w_out, b_w_qkv, b_w_o, rel_bias, f_w_up, f_conv_w, f_conv_b, f_w_down):
    batch, seq, d = x.shape
    depth = ln_mix_g.shape[0]
    alpha = (2 * depth) ** 0.25
    cfg = _tiles(seq)
    tables = _bias_tables(rel_bias)
    h = x.reshape(batch * seq, d)
    for i in range(depth):
        j = i // 2
        if i % 2 == 0:
            h = _gmlp_layer(h, a_w_in[j], a_ln_g[j], a_ln_b[j], a_w_s[j], a_b_s[j], a_w_out[j],
                            ln_mix_g[i], ln_mix_b[i], alpha=alpha, tm=cfg["tm_mix"])
        else:
            q2, kaug, v2, kbar = _qkv_proj(h, b_w_qkv[j].astype(BF16), batch=batch, seq=seq,
                                           tm=cfg["tm_qkv"], tn=cfg["tn_qkv"])
            attn = _moba_attention(q2, kaug, v2, kbar, tables, batch=batch, seq=seq)
            h = _out_proj_layer(attn.reshape(batch * seq, d), b_w_o[j], h,
                                ln_mix_g[i], ln_mix_b[i], alpha=alpha, tm=cfg["tm_mix"])
        h = _ffn_layer(h, f_w_up[i], f_conv_w[i], f_conv_b[i], f_w_down[i],
                       ln_ffn_g[i], ln_ffn_b[i], alpha=alpha, tm=cfg["tm_ffn"],
                       fc=cfg["fc_ffn"], seq=seq)
    return h.reshape(batch, seq, d)
```

```python
import functools
import math

import jax
import jax.numpy as jnp
from jax import lax
from jax.experimental import pallas as pl
from jax.experimental.pallas import tpu as pltpu

F32 = jnp.float32
BF16 = jnp.bfloat16

GMLP_CHUNK = 128
N_HEADS = 16
MOBA_BLOCK = 256
MOBA_TOPK = 3
REL_BUCKETS = 32
REL_MAX_DIST = 128
LN_EPS = 1e-5

LANES_V7X = 128
SUBLANES_V7X = 8
BF16_SUBLANES_V7X = 16
VMEM_LIMIT_BYTES_V7X = 56 * 1024 * 1024

MASK_NEG = -1e30


def _layer_norm(x, g, b):
    mu = jnp.mean(x, axis=-1, keepdims=True)
    xc = x - mu
    var = jnp.mean(xc * xc, axis=-1, keepdims=True)
    return xc * lax.rsqrt(var + LN_EPS) * g + b


def _resident(shape):
    return pl.BlockSpec(shape, lambda *_: (0,) * len(shape), pipeline_mode=pl.Buffered(1))


def _params(n_axes):
    return pltpu.CompilerParams(
        dimension_semantics=("arbitrary",) * n_axes,
        vmem_limit_bytes=VMEM_LIMIT_BYTES_V7X,
    )


def _gmlp_kernel(x_ref, w_in_ref, lng_ref, lnb_ref, ws_ref, bs_ref, w_out_ref,
                 g_ref, b_ref, o_ref, *, alpha, n_groups):
    x = x_ref[...]
    tm = x.shape[0]
    width = w_out_ref.shape[0]
    gdim = width // n_groups
    z = jnp.dot(x.astype(BF16), w_in_ref[...], preferred_element_type=F32)
    z = jax.nn.gelu(z)
    u = z[:, :width]
    v = _layer_norm(z[:, width:], lng_ref[...], lnb_ref[...]).astype(BF16)
    row = lax.broadcasted_iota(jnp.int32, (GMLP_CHUNK, GMLP_CHUNK), 0)
    col = lax.broadcasted_iota(jnp.int32, (GMLP_CHUNK, GMLP_CHUNK), 1)
    causal = col <= row
    w_masked = [jnp.where(causal, ws_ref[g], 0.0).astype(BF16) for g in range(n_groups)]
    chunks = []
    for c in range(tm // GMLP_CHUNK):
        rows = slice(c * GMLP_CHUNK, (c + 1) * GMLP_CHUNK)
        groups = [
            jnp.dot(w_masked[g], v[rows, g * gdim:(g + 1) * gdim], preferred_element_type=F32)
            for g in range(n_groups)
        ]
        chunks.append(jnp.concatenate(groups, axis=1) + bs_ref[...])
    sv = jnp.concatenate(chunks, axis=0)
    y = jnp.dot((u * sv).astype(BF16), w_out_ref[...], preferred_element_type=F32)
    o_ref[...] = _layer_norm(alpha * x + y, g_ref[...], b_ref[...])


def _gmlp_layer(x, w_in, ln_g, ln_b, w_s, b_s, w_out, g, b, *, alpha, tm):
    t, d = x.shape
    width = w_out.shape[0]
    n_groups = w_s.shape[0]
    bs_tile = jnp.repeat(b_s.T, width // n_groups, axis=1)
    row_spec = pl.BlockSpec((tm, d), lambda i: (i, 0))
    return pl.pallas_call(
        functools.partial(_gmlp_kernel, alpha=alpha, n_groups=n_groups),
        grid=(t // tm,),
        in_specs=[
            row_spec,
            _resident(w_in.shape),
            _resident((1, width)), _resident((1, width)),
            _resident(w_s.shape),
            _resident(bs_tile.shape),
            _resident(w_out.shape),
            _resident((1, d)), _resident((1, d)),
        ],
        out_specs=row_spec,
        out_shape=jax.ShapeDtypeStruct((t, d), F32),
        compiler_params=_params(1),
        name="gmlp_layer",
    )(x, w_in.astype(BF16), ln_g[None], ln_b[None], w_s, bs_tile, w_out.astype(BF16),
      g[None], b[None])


def _causal_conv(h, prev, cw, cb):
    def taps(cur, back1, back2):
        return cb + cw[0:1] * back2 + cw[1:2] * back1 + cw[2:3] * cur

    body = taps(h, pltpu.roll(h, 1, 0), pltpu.roll(h, 2, 0))
    head = jnp.concatenate([prev, h[:SUBLANES_V7X]], axis=0)
    top = taps(head, pltpu.roll(head, 1, 0), pltpu.roll(head, 2, 0))[SUBLANES_V7X:]
    return jnp.concatenate([top, body[SUBLANES_V7X:]], axis=0)


def _ffn_kernel(x_ref, w_up_ref, cw_ref, cb_ref, w_down_ref, g_ref, b_ref, o_ref,
                act_ref, carry_ref, *, alpha, fc, tiles_per_seq):
    x = x_ref[...]
    tm = x.shape[0]
    ffn = w_down_ref.shape[0]

    @pl.when(pl.program_id(0) % tiles_per_seq == 0)
    def _():
        carry_ref[...] = jnp.zeros_like(carry_ref)

    xb = x.astype(BF16)

    def conv_cols(start):
        cols = slice(start, start + fc)
        h = jnp.dot(xb, w_up_ref[:, cols], preferred_element_type=F32)
        prev = carry_ref[:, cols]
        carry_ref[:, cols] = h[tm - SUBLANES_V7X:]
        return _causal_conv(h, prev, cw_ref[:, cols], cb_ref[:, cols])

    for j in range(ffn // fc):
        gate = conv_cols(j * fc)
        val = conv_cols(ffn + j * fc)
        act_ref[:, j * fc:(j + 1) * fc] = (jax.nn.gelu(gate) * val).astype(BF16)

    y = jnp.dot(act_ref[...], w_down_ref[...], preferred_element_type=F32)
    o_ref[...] = _layer_norm(alpha * x + y, g_ref[...], b_ref[...])


def _ffn_layer(x, w_up, conv_w, conv_b, w_down, g, b, *, alpha, tm, fc, seq):
    t, d = x.shape
    ffn = w_down.shape[0]
    row_spec = pl.BlockSpec((tm, d), lambda i: (i, 0))
    return pl.pallas_call(
        functools.partial(_ffn_kernel, alpha=alpha, fc=fc, tiles_per_seq=seq // tm),
        grid=(t // tm,),
        in_specs=[
            row_spec,
            _resident(w_up.shape),
            _resident(conv_w.shape),
            _resident((1, 2 * ffn)),
            _resident(w_down.shape),
            _resident((1, d)), _resident((1, d)),
        ],
        out_specs=row_spec,
        out_shape=jax.ShapeDtypeStruct((t, d), F32),
        scratch_shapes=[
            pltpu.VMEM((tm, ffn), BF16),
            pltpu.VMEM((SUBLANES_V7X, 2 * ffn), F32),
        ],
        compiler_params=_params(1),
        name="conv_ffn_layer",
    )(x, w_up.astype(BF16), conv_w, conv_b[None], w_down.astype(BF16), g[None], b[None])


def _qkv_kernel(x_ref, wq_ref, wk_ref, wv_ref, q_ref, k_ref, v_ref, kbar_ref,
                *, head_dim, tiles_per_seq):
    q_scale = head_dim ** -0.5
    xb = x_ref[...].astype(BF16)
    tm = xb.shape[0]
    q = jnp.dot(xb, wq_ref[...], preferred_element_type=F32) * q_scale
    k = jnp.dot(xb, wk_ref[...], preferred_element_type=F32)
    v = jnp.dot(xb, wv_ref[...], preferred_element_type=F32)
    pos = (pl.program_id(0) % tiles_per_seq) * tm + lax.broadcasted_iota(
        jnp.int32, (tm, LANES_V7X), 0)
    lane = lax.broadcasted_iota(jnp.int32, (tm, LANES_V7X), 1)
    block_onehot = jnp.where(pos // MOBA_BLOCK == lane, 1.0, 0.0).astype(BF16)
    n_pairs = q.shape[1] // LANES_V7X
    q_t = q.T.astype(BF16)
    v_t = v.T.astype(BF16)
    for p in range(n_pairs):
        cols = slice(p * LANES_V7X, (p + 1) * LANES_V7X)
        q_ref[0, p] = q_t[cols]
        k_ref[0, p, :, :LANES_V7X] = k[:, cols].astype(BF16)
        k_ref[0, p, :, LANES_V7X:] = block_onehot
    n_heads = q.shape[1] // head_dim
    for h in range(n_heads):
        v_ref[0, h, :head_dim] = v_t[h * head_dim:(h + 1) * head_dim]
        v_ref[0, h, head_dim:] = jnp.ones((v_ref.shape[2] - head_dim, tm), BF16)
    kbar_ref[0] = jnp.mean(k.reshape(tm // MOBA_BLOCK, MOBA_BLOCK, k.shape[1]), axis=1)


def _qkv_proj(x, w_qkv, *, batch, seq, tm, tn):
    t, d = x.shape
    n_pairs = d // LANES_V7X
    pairs_per_step = tn // LANES_V7X
    tiles_per_seq = seq // tm
    n_col = d // tn
    head_dim = d // N_HEADS
    heads_per_step = tn // head_dim
    v_rows = head_dim + BF16_SUBLANES_V7X

    def row_major_map(i, j):
        return (i // tiles_per_seq, j, i % tiles_per_seq, 0)

    def feature_major_map(i, j):
        return (i // tiles_per_seq, j, 0, i % tiles_per_seq)

    return pl.pallas_call(
        functools.partial(_qkv_kernel, head_dim=head_dim, tiles_per_seq=tiles_per_seq),
        grid=(t // tm, n_col),
        in_specs=[
            pl.BlockSpec((tm, d), lambda i, j: (i, 0)),
            pl.BlockSpec((d, tn), lambda i, j: (0, j)),
            pl.BlockSpec((d, tn), lambda i, j: (0, n_col + j)),
            pl.BlockSpec((d, tn), lambda i, j: (0, 2 * n_col + j)),
        ],
        out_specs=[
            pl.BlockSpec((1, pairs_per_step, LANES_V7X, tm), feature_major_map),
            pl.BlockSpec((1, pairs_per_step, tm, 2 * LANES_V7X), row_major_map),
            pl.BlockSpec((1, heads_per_step, v_rows, tm), feature_major_map),
            pl.BlockSpec((1, tm // MOBA_BLOCK, tn),
                         lambda i, j: (i // tiles_per_seq, i % tiles_per_seq, j)),
        ],
        out_shape=[
            jax.ShapeDtypeStruct((batch, n_pairs, LANES_V7X, seq), BF16),
            jax.ShapeDtypeStruct((batch, n_pairs, seq, 2 * LANES_V7X), BF16),
            jax.ShapeDtypeStruct((batch, N_HEADS, v_rows, seq), BF16),
            jax.ShapeDtypeStruct((batch, seq // MOBA_BLOCK, d), F32),
        ],
        compiler_params=_params(2),
        name="moba_qkv_proj",
    )(x, w_qkv, w_qkv, w_qkv)


def _moba_attn_kernel(q_ref, k_ref, v_ref, kbar_ref, tab_ref, o_ref,
                      s_ref, cmax_ref, acc_ref, *, head_dim):
    tile = pl.program_id(2)
    blk_len = MOBA_BLOCK
    q_t = q_ref[0, 0]
    tq = q_t.shape[1]
    n_blocks = kbar_ref.shape[1]
    feat_row = lax.broadcasted_iota(jnp.int32, q_t.shape, 0)
    kbar = kbar_ref[0].astype(BF16)
    blk = lax.broadcasted_iota(jnp.int32, (n_blocks, tq), 0)
    query = lax.broadcasted_iota(jnp.int32, (n_blocks, tq), 1)
    cur = 2 * tile + jnp.where(query >= blk_len, 1, 0)

    q_aug = []
    for h in range(2):
        in_head = (feat_row >= head_dim) if h else (feat_row < head_dim)
        qm = jnp.where(in_head, q_t, jnp.zeros_like(q_t))
        gate = jnp.dot(kbar, qm, preferred_element_type=F32)
        gate = jnp.where(blk < cur, gate, -jnp.inf)
        rank = jnp.zeros(gate.shape, F32)
        for m in range(n_blocks):
            gm = gate[m:m + 1, :]
            tie = jnp.where(blk > m, 1.0, 0.0)
            rank = rank + jnp.where(gm > gate, 1.0, jnp.where(gm == gate, tie, 0.0))
        feat = jnp.where(blk < cur,
                         jnp.where(rank < MOBA_TOPK, 0.0, MASK_NEG),
                         jnp.where(blk == cur, 0.0, MASK_NEG))
        feat = jnp.concatenate(
            [feat, jnp.zeros((LANES_V7X - n_blocks, tq), F32)], axis=0)
        q_aug.append(jnp.concatenate([qm, feat.astype(BF16)], axis=0))
    q_aug = jnp.concatenate(q_aug, axis=1)
    width = q_aug.shape[1]
    pair_len = 2 * blk_len

    def scores(pair):
        start = pl.multiple_of(pair * pair_len, pair_len)
        return jnp.dot(k_ref[0, 0, pl.ds(start, pair_len), :], q_aug,
                       preferred_element_type=F32)

    def keep(pair, s):
        s_ref[pair] = s
        return jnp.max(s.reshape(pair_len // SUBLANES_V7X, SUBLANES_V7X, width), axis=0)

    cmax_ref[...] = keep(tile, scores(tile) + tab_ref[0, blk_len:])

    @pl.when(tile > 0)
    def _():
        s = scores(tile - 1)
        s = jnp.concatenate([s[:blk_len], s[blk_len:] + tab_ref[0, :blk_len]], axis=0)
        cmax_ref[...] = jnp.maximum(cmax_ref[...], keep(tile - 1, s))

    cmax_ref[...] = lax.fori_loop(
        0, jnp.maximum(tile - 1, 0),
        lambda i, c: jnp.maximum(c, keep(i, scores(i))), cmax_ref[...])
    col_m = jnp.max(cmax_ref[...], axis=0, keepdims=True)

    acc_ref[...] = jnp.zeros_like(acc_ref)

    def accumulate(pair, carry):
        p = jnp.exp((s_ref[pair] - col_m).astype(BF16))
        start = pl.multiple_of(pair * pair_len, pair_len)
        for h in range(2):
            acc_ref[h] += jnp.dot(v_ref[0, h, :, pl.ds(start, pair_len)],
                                  p[:, h * tq:(h + 1) * tq], preferred_element_type=F32)
        return carry

    lax.fori_loop(0, tile + 1, accumulate, 0)

    heads = []
    for h in range(2):
        acc = acc_ref[h]
        heads.append(acc[:head_dim] / acc[head_dim:head_dim + 1])
    o_ref[0] = jnp.concatenate(heads, axis=0).T.astype(o_ref.dtype)


def _rel_bucket(dist):
    n = jnp.maximum(dist, 0)
    max_exact = REL_BUCKETS // 2
    nf = jnp.maximum(n, 1).astype(F32)
    large = max_exact + (jnp.log(nf / max_exact) / math.log(REL_MAX_DIST / max_exact)
                         * (REL_BUCKETS - max_exact)).astype(jnp.int32)
    large = jnp.minimum(large, REL_BUCKETS - 1)
    return jnp.where(n < max_exact, n, large)


def _bias_tables(rel_bias):
    blk = MOBA_BLOCK
    n_keys, n_q = 3 * blk, 2 * blk
    n_heads = rel_bias.shape[1]
    bias_h = rel_bias.T
    d_min = blk - (n_keys - 1)
    dist = jnp.arange(d_min, blk + n_q)
    by_dist = bias_h[:, _rel_bucket(dist)] - bias_h[:, REL_BUCKETS - 1:]
    by_dist = jnp.where(dist[None] < 0, MASK_NEG, by_dist)
    period = by_dist.shape[1] + 1
    rows = jnp.pad(by_dist, ((0, 0), (0, 1)))
    skew = jnp.broadcast_to(rows[:, None, :], (n_heads, n_keys, period))
    skew = skew.reshape(n_heads, n_keys * period)[:, :n_keys * (period - 1)]
    toeplitz = skew.reshape(n_heads, n_keys, period - 1)[:, :, blk - d_min:]
    tables = toeplitz.reshape(n_heads // 2, 2, n_keys, n_q)
    return tables.transpose(0, 2, 1, 3).reshape(n_heads // 2, n_keys, 2 * n_q)


def _moba_attention(q_t, kaug, v_t, kbar, tables, *, batch, seq):
    n_pairs = q_t.shape[1]
    d = n_pairs * LANES_V7X
    tq = 2 * MOBA_BLOCK
    n_blocks = seq // MOBA_BLOCK
    v_rows = v_t.shape[2]
    return pl.pallas_call(
        functools.partial(_moba_attn_kernel, head_dim=d // N_HEADS),
        grid=(n_pairs, batch, seq // tq),
        in_specs=[
            pl.BlockSpec((1, 1, LANES_V7X, tq), lambda p, b, t: (b, p, 0, t)),
            pl.BlockSpec((1, 1, seq, 2 * LANES_V7X), lambda p, b, t: (b, p, 0, 0)),
            pl.BlockSpec((1, 2, v_rows, seq), lambda p, b, t: (b, p, 0, 0)),
            pl.BlockSpec((1, n_blocks, LANES_V7X), lambda p, b, t: (b, 0, p)),
            pl.BlockSpec((1,) + tables.shape[1:], lambda p, b, t: (p, 0, 0)),
        ],
        out_specs=pl.BlockSpec((1, tq, LANES_V7X), lambda p, b, t: (b, t, p)),
        out_shape=jax.ShapeDtypeStruct((batch, seq, d), BF16),
        scratch_shapes=[
            pltpu.VMEM((seq // tq, tq, 2 * tq), F32),
            pltpu.VMEM((SUBLANES_V7X, 2 * tq), F32),
            pltpu.VMEM((2, v_rows, tq), F32),
        ],
        compiler_params=_params(3),
        name="moba_attention",
    )(q_t, kaug, v_t, kbar, tables)


def _out_proj_kernel(a_ref, w_ref, x_ref, g_ref, b_ref, o_ref, *, alpha):
    y = jnp.dot(a_ref[...], w_ref[...], preferred_element_type=F32)
    o_ref[...] = _layer_norm(alpha * x_ref[...] + y, g_ref[...], b_ref[...])


def _out_proj_layer(a, w_o, x, g, b, *, alpha, tm):
    t, d = x.shape
    row_spec = pl.BlockSpec((tm, d), lambda i: (i, 0))
    return pl.pallas_call(
        functools.partial(_out_proj_kernel, alpha=alpha),
        grid=(t // tm,),
        in_specs=[row_spec, _resident(w_o.shape), row_spec, _resident((1, d)), _resident((1, d))],
        out_specs=row_spec,
        out_shape=jax.ShapeDtypeStruct((t, d), F32),
        compiler_params=_params(1),
        name="moba_out_proj",
    )(a, w_o.astype(BF16), x, g[None], b[None])


def _tiles(seq):
    assert seq % (SUBLANES_V7X * MOBA_BLOCK) == 0, "qkv tile must hold 8 whole key blocks"
    return dict(tm_mix=min(seq, 512), tm_ffn=min(seq, 512), tm_qkv=SUBLANES_V7X * MOBA_BLOCK,
                tn_qkv=2 * LANES_V7X, fc_ffn=2 * LANES_V7X)


def kernel(x, ln_mix_g, ln_mix_b, ln_ffn_g, ln_ffn_b, a_w_in, a_ln_g, a_ln_b, a_w_s, a_b_s,
           a_w_out, b_w_qkv, b_w_o, rel_bias, f_w_up, f_conv_w, f_conv_b, f_w_down):
    batch, seq, d = x.shape
    depth = ln_mix_g.shape[0]
    alpha = (2 * depth) ** 0.25
    cfg = _tiles(seq)
    tables = _bias_tables(rel_bias)
    h = x.reshape(batch * seq, d)
    for i in range(depth):
        j = i // 2
        if i % 2 == 0:
            h = _gmlp_layer(h, a_w_in[j], a_ln_g[j], a_ln_b[j], a_w_s[j], a_b_s[j], a_w_out[j],
                            ln_mix_g[i], ln_mix_b[i], alpha=alpha, tm=cfg["tm_mix"])
        else:
            q2, kaug, v2, kbar = _qkv_proj(h, b_w_qkv[j].astype(BF16), batch=batch, seq=seq,
                                           tm=cfg["tm_qkv"], tn=cfg["tn_qkv"])
            attn = _moba_attention(q2, kaug, v2, kbar, tables, batch=batch, seq=seq)
            h = _out_proj_layer(attn.reshape(batch * seq, d), b_w_o[j], h,
                                ln_mix_g[i], ln_mix_b[i], alpha=alpha, tm=cfg["tm_mix"])
        h = _ffn_layer(h, f_w_up[i], f_conv_w[i], f_conv_b[i], f_w_down[i],
                       ln_ffn_g[i], ln_ffn_b[i], alpha=alpha, tm=cfg["tm_ffn"],
                       fc=cfg["fc_ffn"], seq=seq)
    return h.reshape(batch, seq, d)
```

```python
import functools
import math

import jax
import jax.numpy as jnp
from jax import lax
from jax.experimental import pallas as pl
from jax.experimental.pallas import tpu as pltpu

F32 = jnp.float32
BF16 = jnp.bfloat16

GMLP_CHUNK = 128
N_HEADS = 16
MOBA_BLOCK = 256
MOBA_TOPK = 3
REL_BUCKETS = 32
REL_MAX_DIST = 128
LN_EPS = 1e-5

LANES_V7X = 128
SUBLANES_V7X = 8
BF16_SUBLANES_V7X = 16
VMEM_LIMIT_BYTES_V7X = 56 * 1024 * 1024

MASK_NEG = -1e30


def _layer_norm(x, g, b):
    mu = jnp.mean(x, axis=-1, keepdims=True)
    xc = x - mu
    var = jnp.mean(xc * xc, axis=-1, keepdims=True)
    return xc * lax.rsqrt(var + LN_EPS) * g + b


def _resident(shape):
    return pl.BlockSpec(shape, lambda *_: (0,) * len(shape), pipeline_mode=pl.Buffered(1))


def _params(n_axes):
    return pltpu.CompilerParams(
        dimension_semantics=("arbitrary",) * n_axes,
        vmem_limit_bytes=VMEM_LIMIT_BYTES_V7X,
    )


def _gmlp_kernel(x_ref, w_in_ref, lng_ref, lnb_ref, ws_ref, bs_ref, w_out_ref,
                 g_ref, b_ref, o_ref, *, alpha, n_groups):
    x = x_ref[...]
    tm = x.shape[0]
    width = w_out_ref.shape[0]
    gdim = width // n_groups
    z = jnp.dot(x.astype(BF16), w_in_ref[...], preferred_element_type=F32)
    z = jax.nn.gelu(z)
    u = z[:, :width]
    v = _layer_norm(z[:, width:], lng_ref[...], lnb_ref[...]).astype(BF16)
    row = lax.broadcasted_iota(jnp.int32, (GMLP_CHUNK, GMLP_CHUNK), 0)
    col = lax.broadcasted_iota(jnp.int32, (GMLP_CHUNK, GMLP_CHUNK), 1)
    causal = col <= row
    w_masked = [jnp.where(causal, ws_ref[g], 0.0).astype(BF16) for g in range(n_groups)]
    chunks = []
    for c in range(tm // GMLP_CHUNK):
        rows = slice(c * GMLP_CHUNK, (c + 1) * GMLP_CHUNK)
        groups = [
            jnp.dot(w_masked[g], v[rows, g * gdim:(g + 1) * gdim], preferred_element_type=F32)
            for g in range(n_groups)
        ]
        chunks.append(jnp.concatenate(groups, axis=1) + bs_ref[...])
    sv = jnp.concatenate(chunks, axis=0)
    y = jnp.dot((u * sv).astype(BF16), w_out_ref[...], preferred_element_type=F32)
    o_ref[...] = _layer_norm(alpha * x + y, g_ref[...], b_ref[...])


def _gmlp_layer(x, w_in, ln_g, ln_b, w_s, b_s, w_out, g, b, *, alpha, tm):
    t, d = x.shape
    width = w_out.shape[0]
    n_groups = w_s.shape[0]
    bs_tile = jnp.repeat(b_s.T, width // n_groups, axis=1)
    row_spec = pl.BlockSpec((tm, d), lambda i: (i, 0))
    return pl.pallas_call(
        functools.partial(_gmlp_kernel, alpha=alpha, n_groups=n_groups),
        grid=(t // tm,),
        in_specs=[
            row_spec,
            _resident(w_in.shape),
            _resident((1, width)), _resident((1, width)),
            _resident(w_s.shape),
            _resident(bs_tile.shape),
            _resident(w_out.shape),
            _resident((1, d)), _resident((1, d)),
        ],
        out_specs=row_spec,
        out_shape=jax.ShapeDtypeStruct((t, d), F32),
        compiler_params=_params(1),
        name="gmlp_layer",
    )(x, w_in.astype(BF16), ln_g[None], ln_b[None], w_s, bs_tile, w_out.astype(BF16),
      g[None], b[None])


def _causal_conv(h, prev, cw, cb):
    def taps(cur, back1, back2):
        return cb + cw[0:1] * back2 + cw[1:2] * back1 + cw[2:3] * cur

    body = taps(h, pltpu.roll(h, 1, 0), pltpu.roll(h, 2, 0))
    head = jnp.concatenate([prev, h[:SUBLANES_V7X]], axis=0)
    top = taps(head, pltpu.roll(head, 1, 0), pltpu.roll(head, 2, 0))[SUBLANES_V7X:]
    return jnp.concatenate([top, body[SUBLANES_V7X:]], axis=0)


def _ffn_kernel(x_ref, w_up_ref, cw_ref, cb_ref, w_down_ref, g_ref, b_ref, o_ref,
                act_ref, carry_ref, *, alpha, fc, tiles_per_seq):
    x = x_ref[...]
    tm = x.shape[0]
    ffn = w_down_ref.shape[0]

    @pl.when(pl.program_id(0) % tiles_per_seq == 0)
    def _():
        carry_ref[...] = jnp.zeros_like(carry_ref)

    xb = x.astype(BF16)

    def conv_cols(start):
        cols = slice(start, start + fc)
        h = jnp.dot(xb, w_up_ref[:, cols], preferred_element_type=F32)
        prev = carry_ref[:, cols]
        carry_ref[:, cols] = h[tm - SUBLANES_V7X:]
        return _causal_conv(h, prev, cw_ref[:, cols], cb_ref[:, cols])

    for j in range(ffn // fc):
        gate = conv_cols(j * fc)
        val = conv_cols(ffn + j * fc)
        act_ref[:, j * fc:(j + 1) * fc] = (jax.nn.gelu(gate) * val).astype(BF16)

    y = jnp.dot(act_ref[...], w_down_ref[...], preferred_element_type=F32)
    o_ref[...] = _layer_norm(alpha * x + y, g_ref[...], b_ref[...])


def _ffn_layer(x, w_up, conv_w, conv_b, w_down, g, b, *, alpha, tm, fc, seq):
    t, d = x.shape
    ffn = w_down.shape[0]
    row_spec = pl.BlockSpec((tm, d), lambda i: (i, 0))
    return pl.pallas_call(
        functools.partial(_ffn_kernel, alpha=alpha, fc=fc, tiles_per_seq=seq // tm),
        grid=(t // tm,),
        in_specs=[
            row_spec,
            _resident(w_up.shape),
            _resident(conv_w.shape),
            _resident((1, 2 * ffn)),
            _resident(w_down.shape),
            _resident((1, d)), _resident((1, d)),
        ],
        out_specs=row_spec,
        out_shape=jax.ShapeDtypeStruct((t, d), F32),
        scratch_shapes=[
            pltpu.VMEM((tm, ffn), BF16),
            pltpu.VMEM((SUBLANES_V7X, 2 * ffn), F32),
        ],
        compiler_params=_params(1),
        name="conv_ffn_layer",
    )(x, w_up.astype(BF16), conv_w, conv_b[None], w_down.astype(BF16), g[None], b[None])


def _qkv_kernel(x_ref, wq_ref, wk_ref, wv_ref, q_ref, k_ref, v_ref, kbar_ref,
                *, head_dim, tiles_per_seq):
    q_scale = head_dim ** -0.5
    xb = x_ref[...].astype(BF16)
    tm = xb.shape[0]
    q = jnp.dot(xb, wq_ref[...], preferred_element_type=F32) * q_scale
    k = jnp.dot(xb, wk_ref[...], preferred_element_type=F32)
    v = jnp.dot(xb, wv_ref[...], preferred_element_type=F32)
    pos = (pl.program_id(0) % tiles_per_seq) * tm + lax.broadcasted_iota(
        jnp.int32, (tm, LANES_V7X), 0)
    lane = lax.broadcasted_iota(jnp.int32, (tm, LANES_V7X), 1)
    block_onehot = jnp.where(pos // MOBA_BLOCK == lane, 1.0, 0.0).astype(BF16)
    n_pairs = q.shape[1] // LANES_V7X
    q_t = q.T.astype(BF16)
    v_t = v.T.astype(BF16)
    for p in range(n_pairs):
        cols = slice(p * LANES_V7X, (p + 1) * LANES_V7X)
        q_ref[0, p] = q_t[cols]
        k_ref[0, p, :, :LANES_V7X] = k[:, cols].astype(BF16)
        k_ref[0, p, :, LANES_V7X:] = block_onehot
    n_heads = q.shape[1] // head_dim
    for h in range(n_heads):
        v_ref[0, h, :head_dim] = v_t[h * head_dim:(h + 1) * head_dim]
        v_ref[0, h, head_dim:] = jnp.ones((v_ref.shape[2] - head_dim, tm), BF16)
    kbar_ref[0] = jnp.mean(k.reshape(tm // MOBA_BLOCK, MOBA_BLOCK, k.shape[1]), axis=1)


def _qkv_proj(x, w_qkv, *, batch, seq, tm, tn):
    t, d = x.shape
    n_pairs = d // LANES_V7X
    pairs_per_step = tn // LANES_V7X
    tiles_per_seq = seq // tm
    n_col = d // tn
    head_dim = d // N_HEADS
    heads_per_step = tn // head_dim
    v_rows = head_dim + BF16_SUBLANES_V7X

    def row_major_map(i, j):
        return (i // tiles_per_seq, j, i % tiles_per_seq, 0)

    def feature_major_map(i, j):
        return (i // tiles_per_seq, j, 0, i % tiles_per_seq)

    return pl.pallas_call(
        functools.partial(_qkv_kernel, head_dim=head_dim, tiles_per_seq=tiles_per_seq),
        grid=(t // tm, n_col),
        in_specs=[
            pl.BlockSpec((tm, d), lambda i, j: (i, 0)),
            pl.BlockSpec((d, tn), lambda i, j: (0, j)),
            pl.BlockSpec((d, tn), lambda i, j: (0, n_col + j)),
            pl.BlockSpec((d, tn), lambda i, j: (0, 2 * n_col + j)),
        ],
        out_specs=[
            pl.BlockSpec((1, pairs_per_step, LANES_V7X, tm), feature_major_map),
            pl.BlockSpec((1, pairs_per_step, tm, 2 * LANES_V7X), row_major_map),
            pl.BlockSpec((1, heads_per_step, v_rows, tm), feature_major_map),
            pl.BlockSpec((1, tm // MOBA_BLOCK, tn),
                         lambda i, j: (i // tiles_per_seq, i % tiles_per_seq, j)),
        ],
        out_shape=[
            jax.ShapeDtypeStruct((batch, n_pairs, LANES_V7X, seq), BF16),
            jax.ShapeDtypeStruct((batch, n_pairs, seq, 2 * LANES_V7X), BF16),
            jax.ShapeDtypeStruct((batch, N_HEADS, v_rows, seq), BF16),
            jax.ShapeDtypeStruct((batch, seq // MOBA_BLOCK, d), F32),
        ],
        compiler_params=_params(2),
        name="moba_qkv_proj",
    )(x, w_qkv, w_qkv, w_qkv)


def _moba_attn_kernel(q_ref, k_ref, v_ref, kbar_ref, tab_ref, o_ref,
                      s_ref, cmax_ref, acc_ref, *, head_dim):
    tile = pl.program_id(2)
    blk_len = MOBA_BLOCK
    q_t = q_ref[0, 0]
    tq = q_t.shape[1]
    n_blocks = kbar_ref.shape[1]
    feat_row = lax.broadcasted_iota(jnp.int32, q_t.shape, 0)
    kbar = kbar_ref[0].astype(BF16)
    blk = lax.broadcasted_iota(jnp.int32, (n_blocks, tq), 0)
    query = lax.broadcasted_iota(jnp.int32, (n_blocks, tq), 1)
    cur = 2 * tile + jnp.where(query >= blk_len, 1, 0)

    q_aug = []
    for h in range(2):
        in_head = (feat_row >= head_dim) if h else (feat_row < head_dim)
        qm = jnp.where(in_head, q_t, jnp.zeros_like(q_t))
        gate = jnp.dot(kbar, qm, preferred_element_type=F32)
        gate = jnp.where(blk < cur, gate, -jnp.inf)
        rank = jnp.zeros(gate.shape, F32)
        for m in range(n_blocks):
            gm = gate[m:m + 1, :]
            tie = jnp.where(blk > m, 1.0, 0.0)
            rank = rank + jnp.where(gm > gate, 1.0, jnp.where(gm == gate, tie, 0.0))
        feat = jnp.where(blk < cur,
                         jnp.where(rank < MOBA_TOPK, 0.0, MASK_NEG),
                         jnp.where(blk == cur, 0.0, MASK_NEG))
        feat = jnp.concatenate(
            [feat, jnp.zeros((LANES_V7X - n_blocks, tq), F32)], axis=0)
        q_aug.append(jnp.concatenate([qm, feat.astype(BF16)], axis=0))
    q_aug = jnp.concatenate(q_aug, axis=1)
    width = q_aug.shape[1]
    pair_len = 2 * blk_len

    def scores(pair):
        start = pl.multiple_of(pair * pair_len, pair_len)
        return jnp.dot(k_ref[0, 0, pl.ds(start, pair_len), :], q_aug,
                       preferred_element_type=F32)

    def keep(pair, s):
        s_ref[pair] = s
        return jnp.max(s.reshape(pair_len // SUBLANES_V7X, SUBLANES_V7X, width), axis=0)

    def own_pair():
        return keep(tile, scores(tile) + tab_ref[0, blk_len:])

    @pl.when(tile == 0)
    def _():
        cmax_ref[...] = own_pair()

    @pl.when(tile > 0)
    def _():
        s = scores(tile - 1)
        s = jnp.concatenate([s[:blk_len], s[blk_len:] + tab_ref[0, :blk_len]], axis=0)
        cmax_ref[...] = jnp.maximum(own_pair(), keep(tile - 1, s))

    n_far = jnp.maximum(tile - 1, 0)
    cmax_ref[...] = lax.fori_loop(
        0, n_far // 2,
        lambda i, c: jnp.maximum(c, jnp.maximum(keep(2 * i, scores(2 * i)),
                                                keep(2 * i + 1, scores(2 * i + 1)))),
        cmax_ref[...])

    @pl.when(n_far % 2 == 1)
    def _():
        cmax_ref[...] = jnp.maximum(cmax_ref[...], keep(n_far - 1, scores(n_far - 1)))

    col_m = jnp.max(cmax_ref[...], axis=0, keepdims=True)

    acc_ref[...] = jnp.zeros_like(acc_ref)

    def accumulate(first, count):
        s = s_ref[pl.ds(first, count)].reshape(count * pair_len, width)
        p = jnp.exp((s - col_m).astype(BF16))
        start = pl.multiple_of(first * pair_len, pair_len)
        for h in range(2):
            acc_ref[h] += jnp.dot(v_ref[0, h, :, pl.ds(start, count * pair_len)],
                                  p[:, h * tq:(h + 1) * tq], preferred_element_type=F32)

    def accumulate_two(i, carry):
        accumulate(2 * i, 2)
        return carry

    lax.fori_loop(0, (tile + 1) // 2, accumulate_two, 0)

    @pl.when(tile % 2 == 0)
    def _():
        accumulate(tile, 1)

    heads = []
    for h in range(2):
        acc = acc_ref[h]
        heads.append(acc[:head_dim] / acc[head_dim:head_dim + 1])
    o_ref[0] = jnp.concatenate(heads, axis=0).T.astype(o_ref.dtype)


def _rel_bucket(dist):
    n = jnp.maximum(dist, 0)
    max_exact = REL_BUCKETS // 2
    nf = jnp.maximum(n, 1).astype(F32)
    large = max_exact + (jnp.log(nf / max_exact) / math.log(REL_MAX_DIST / max_exact)
                         * (REL_BUCKETS - max_exact)).astype(jnp.int32)
    large = jnp.minimum(large, REL_BUCKETS - 1)
    return jnp.where(n < max_exact, n, large)


def _bias_tables(rel_bias):
    blk = MOBA_BLOCK
    n_keys, n_q = 3 * blk, 2 * blk
    n_heads = rel_bias.shape[1]
    bias_h = rel_bias.T
    d_min = blk - (n_keys - 1)
    dist = jnp.arange(d_min, blk + n_q)
    by_dist = bias_h[:, _rel_bucket(dist)] - bias_h[:, REL_BUCKETS - 1:]
    by_dist = jnp.where(dist[None] < 0, MASK_NEG, by_dist)
    period = by_dist.shape[1] + 1
    assert period % LANES_V7X == 0 and (period - n_q) % LANES_V7X == 0
    rows = jnp.pad(by_dist, ((0, 0), (0, 1)))[:, None, :]
    shift = (period - n_q) - (blk - d_min)

    def expand(rows_ref, out_ref):
        for h in range(2):
            wide = jnp.broadcast_to(rows_ref[h], (n_keys, period))
            rolled = pltpu.roll(wide, shift, 1, stride=1, stride_axis=0)
            out_ref[0, :, h * n_q:(h + 1) * n_q] = rolled[:, period - n_q:]

    return pl.pallas_call(
        expand,
        grid=(n_heads // 2,),
        in_specs=[pl.BlockSpec((2, 1, period), lambda p: (p, 0, 0))],
        out_specs=pl.BlockSpec((1, n_keys, 2 * n_q), lambda p: (p, 0, 0)),
        out_shape=jax.ShapeDtypeStruct((n_heads // 2, n_keys, 2 * n_q), F32),
        compiler_params=_params(1),
        name="rel_bias_tables",
    )(rows)


def _moba_attention(q_t, kaug, v_t, kbar, tables, *, batch, seq):
    n_pairs = q_t.shape[1]
    d = n_pairs * LANES_V7X
    tq = 2 * MOBA_BLOCK
    n_blocks = seq // MOBA_BLOCK
    v_rows = v_t.shape[2]
    return pl.pallas_call(
        functools.partial(_moba_attn_kernel, head_dim=d // N_HEADS),
        grid=(n_pairs, batch, seq // tq),
        in_specs=[
            pl.BlockSpec((1, 1, LANES_V7X, tq), lambda p, b, t: (b, p, 0, t)),
            pl.BlockSpec((1, 1, seq, 2 * LANES_V7X), lambda p, b, t: (b, p, 0, 0)),
            pl.BlockSpec((1, 2, v_rows, seq), lambda p, b, t: (b, p, 0, 0)),
            pl.BlockSpec((1, n_blocks, LANES_V7X), lambda p, b, t: (b, 0, p)),
            pl.BlockSpec((1,) + tables.shape[1:], lambda p, b, t: (p, 0, 0)),
        ],
        out_specs=pl.BlockSpec((1, tq, LANES_V7X), lambda p, b, t: (b, t, p)),
        out_shape=jax.ShapeDtypeStruct((batch, seq, d), BF16),
        scratch_shapes=[
            pltpu.VMEM((seq // tq, tq, 2 * tq), F32),
            pltpu.VMEM((SUBLANES_V7X, 2 * tq), F32),
            pltpu.VMEM((2, v_rows, tq), F32),
        ],
        compiler_params=_params(3),
        name="moba_attention",
    )(q_t, kaug, v_t, kbar, tables)


def _out_proj_kernel(a_ref, w_ref, x_ref, g_ref, b_ref, o_ref, *, alpha):
    y = jnp.dot(a_ref[...], w_ref[...], preferred_element_type=F32)
    o_ref[...] = _layer_norm(alpha * x_ref[...] + y, g_ref[...], b_ref[...])


def _out_proj_layer(a, w_o, x, g, b, *, alpha, tm):
    t, d = x.shape
    row_spec = pl.BlockSpec((tm, d), lambda i: (i, 0))
    return pl.pallas_call(
        functools.partial(_out_proj_kernel, alpha=alpha),
        grid=(t // tm,),
        in_specs=[row_spec, _resident(w_o.shape), row_spec, _resident((1, d)), _resident((1, d))],
        out_specs=row_spec,
        out_shape=jax.ShapeDtypeStruct((t, d), F32),
        compiler_params=_params(1),
        name="moba_out_proj",
    )(a, w_o.astype(BF16), x, g[None], b[None])


def _tiles(seq):
    assert seq % (SUBLANES_V7X * MOBA_BLOCK) == 0, "qkv tile must hold 8 whole key blocks"
    return dict(tm_mix=min(seq, 512), tm_ffn=min(seq, 512), tm_qkv=SUBLANES_V7X * MOBA_BLOCK,
                tn_qkv=2 * LANES_V7X, fc_ffn=2 * LANES_V7X)


def kernel(x, ln_mix_g, ln_mix_b, ln_ffn_g, ln_ffn_b, a_w_in, a_ln_g, a_ln_b, a_w_s, a_b_s,
           a_w_out, b_w_qkv, b_w_o, rel_bias, f_w_up, f_conv_w, f_conv_b, f_w_down):
    batch, seq, d = x.shape
    depth = ln_mix_g.shape[0]
    alpha = (2 * depth) ** 0.25
    cfg = _tiles(seq)
    tables = _bias_tables(rel_bias)
    h = x.reshape(batch * seq, d)
    for i in range(depth):
        j = i // 2
        if i % 2 == 0:
            h = _gmlp_layer(h, a_w_in[j], a_ln_g[j], a_ln_b[j], a_w_s[j], a_b_s[j], a_w_out[j],
                            ln_mix_g[i], ln_mix_b[i], alpha=alpha, tm=cfg["tm_mix"])
        else:
            q2, kaug, v2, kbar = _qkv_proj(h, b_w_qkv[j].astype(BF16), batch=batch, seq=seq,
                                           tm=cfg["tm_qkv"], tn=cfg["tn_qkv"])
            attn = _moba_attention(q2, kaug, v2, kbar, tables, batch=batch, seq=seq)
            h = _out_proj_layer(attn.reshape(batch * seq, d), b_w_o[j], h,
                                ln_mix_g[i], ln_mix_b[i], alpha=alpha, tm=cfg["tm_mix"])
        h = _ffn_layer(h, f_w_up[i], f_conv_w[i], f_conv_b[i], f_w_down[i],
                       ln_ffn_g[i], ln_ffn_b[i], alpha=alpha, tm=cfg["tm_ffn"],
                       fc=cfg["fc_ffn"], seq=seq)
    return h.reshape(batch, seq, d)
```

```python
import functools
import math

import jax
import jax.numpy as jnp
from jax import lax
from jax.experimental import pallas as pl
from jax.experimental.pallas import tpu as pltpu

F32 = jnp.float32
BF16 = jnp.bfloat16

GMLP_CHUNK = 128
N_HEADS = 16
MOBA_BLOCK = 256
MOBA_TOPK = 3
REL_BUCKETS = 32
REL_MAX_DIST = 128
LN_EPS = 1e-5

LANES_V7X = 128
SUBLANES_V7X = 8
BF16_SUBLANES_V7X = 16
VMEM_LIMIT_BYTES_V7X = 56 * 1024 * 1024

MASK_NEG = -1e30


def _layer_norm(x, g, b):
    mu = jnp.mean(x, axis=-1, keepdims=True)
    xc = x - mu
    var = jnp.mean(xc * xc, axis=-1, keepdims=True)
    return xc * lax.rsqrt(var + LN_EPS) * g + b


def _layer_block(stacked, layer):
    tail = stacked.shape[1:]
    return pl.BlockSpec((None,) + tail, lambda *_: (layer,) + (0,) * len(tail),
                        pipeline_mode=pl.Buffered(1))


def _params(n_axes):
    return pltpu.CompilerParams(
        dimension_semantics=("arbitrary",) * n_axes,
        vmem_limit_bytes=VMEM_LIMIT_BYTES_V7X,
    )


def _gmlp_kernel(x_ref, w_in_ref, lng_ref, lnb_ref, ws_ref, bs_ref, o_ref, *, n_groups):
    x = x_ref[...]
    tm = x.shape[0]
    width = o_ref.shape[1]
    gdim = width // n_groups
    z = jnp.dot(x.astype(BF16), w_in_ref[...], preferred_element_type=F32)
    z = jax.nn.gelu(z)
    u = z[:, :width]
    v = _layer_norm(z[:, width:], lng_ref[...], lnb_ref[...]).astype(BF16)
    row = lax.broadcasted_iota(jnp.int32, (GMLP_CHUNK, GMLP_CHUNK), 0)
    col = lax.broadcasted_iota(jnp.int32, (GMLP_CHUNK, GMLP_CHUNK), 1)
    causal = col <= row
    w_masked = [jnp.where(causal, ws_ref[g], 0.0).astype(BF16) for g in range(n_groups)]
    chunks = []
    for c in range(tm // GMLP_CHUNK):
        rows = slice(c * GMLP_CHUNK, (c + 1) * GMLP_CHUNK)
        groups = [
            jnp.dot(w_masked[g], v[rows, g * gdim:(g + 1) * gdim], preferred_element_type=F32)
            for g in range(n_groups)
        ]
        chunks.append(jnp.concatenate(groups, axis=1) + bs_ref[...])
    sv = jnp.concatenate(chunks, axis=0)
    o_ref[...] = (u * sv).astype(o_ref.dtype)


def _gmlp_mixer(x, w_in, ln_g, ln_b, w_s, bs_tile, layer, *, tm):
    t, d = x.shape
    width = w_in.shape[2] // 2
    return pl.pallas_call(
        functools.partial(_gmlp_kernel, n_groups=w_s.shape[1]),
        grid=(t // tm,),
        in_specs=[pl.BlockSpec((tm, d), lambda i: (i, 0))]
        + [_layer_block(a, layer) for a in (w_in, ln_g, ln_b, w_s, bs_tile)],
        out_specs=pl.BlockSpec((tm, width), lambda i: (i, 0)),
        out_shape=jax.ShapeDtypeStruct((t, width), BF16),
        compiler_params=_params(1),
        name="gmlp_mixer",
    )(x, w_in, ln_g, ln_b, w_s, bs_tile)


def _causal_conv(h, prev, cw, cb):
    def taps(cur, back1, back2):
        return cb + cw[0:1] * back2 + cw[1:2] * back1 + cw[2:3] * cur

    body = taps(h, pltpu.roll(h, 1, 0), pltpu.roll(h, 2, 0))
    head = jnp.concatenate([prev, h[:SUBLANES_V7X]], axis=0)
    top = taps(head, pltpu.roll(head, 1, 0), pltpu.roll(head, 2, 0))[SUBLANES_V7X:]
    return jnp.concatenate([top, body[SUBLANES_V7X:]], axis=0)


def _channel_kernel(mix_ref, x_ref, w_proj_ref, g_mix_ref, b_mix_ref, w_up_ref, cw_ref, cb_ref,
                    w_down_ref, g_ffn_ref, b_ffn_ref, o_ref, act_ref, carry_ref,
                    *, alpha, fc, tiles_per_seq):
    tm = x_ref.shape[0]
    ffn = w_down_ref.shape[0]

    @pl.when(pl.program_id(0) % tiles_per_seq == 0)
    def _():
        carry_ref[...] = jnp.zeros_like(carry_ref)

    y_mix = jnp.dot(mix_ref[...], w_proj_ref[...], preferred_element_type=F32)
    x = _layer_norm(alpha * x_ref[...] + y_mix, g_mix_ref[...], b_mix_ref[...])
    xb = x.astype(BF16)

    def conv_cols(start):
        cols = slice(start, start + fc)
        h = jnp.dot(xb, w_up_ref[:, cols], preferred_element_type=F32)
        prev = carry_ref[:, cols]
        carry_ref[:, cols] = h[tm - SUBLANES_V7X:]
        return _causal_conv(h, prev, cw_ref[:, cols], cb_ref[:, cols])

    for j in range(ffn // fc):
        gate = conv_cols(j * fc)
        val = conv_cols(ffn + j * fc)
        act_ref[:, j * fc:(j + 1) * fc] = (jax.nn.gelu(gate) * val).astype(BF16)

    y = jnp.dot(act_ref[...], w_down_ref[...], preferred_element_type=F32)
    o_ref[...] = _layer_norm(alpha * x + y, g_ffn_ref[...], b_ffn_ref[...])


def _channel_stage(mix, x, w_proj, proj_layer, ln_mix_g, ln_mix_b, w_up, conv_w, conv_b, w_down,
                   ln_ffn_g, ln_ffn_b, layer, *, alpha, tm, fc, seq):
    t, d = x.shape
    ffn = w_down.shape[1]
    row_spec = pl.BlockSpec((tm, d), lambda i: (i, 0))
    return pl.pallas_call(
        functools.partial(_channel_kernel, alpha=alpha, fc=fc, tiles_per_seq=seq // tm),
        grid=(t // tm,),
        in_specs=[pl.BlockSpec((tm, mix.shape[1]), lambda i: (i, 0)), row_spec,
                  _layer_block(w_proj, proj_layer)]
        + [_layer_block(a, layer)
           for a in (ln_mix_g, ln_mix_b, w_up, conv_w, conv_b, w_down, ln_ffn_g, ln_ffn_b)],
        out_specs=row_spec,
        out_shape=jax.ShapeDtypeStruct((t, d), F32),
        scratch_shapes=[
            pltpu.VMEM((tm, ffn), BF16),
            pltpu.VMEM((SUBLANES_V7X, 2 * ffn), F32),
        ],
        compiler_params=_params(1),
        name="channel_stage",
    )(mix, x, w_proj, ln_mix_g, ln_mix_b, w_up, conv_w, conv_b, w_down, ln_ffn_g, ln_ffn_b)


def _qkv_kernel(x_ref, wq_ref, wk_ref, wv_ref, q_ref, k_ref, v_ref, kbar_ref,
                *, head_dim, tiles_per_seq):
    q_scale = head_dim ** -0.5
    xb = x_ref[...].astype(BF16)
    tm = xb.shape[0]
    q = jnp.dot(xb, wq_ref[...], preferred_element_type=F32) * q_scale
    k = jnp.dot(xb, wk_ref[...], preferred_element_type=F32)
    v = jnp.dot(xb, wv_ref[...], preferred_element_type=F32)
    pos = (pl.program_id(0) % tiles_per_seq) * tm + lax.broadcasted_iota(
        jnp.int32, (tm, LANES_V7X), 0)
    lane = lax.broadcasted_iota(jnp.int32, (tm, LANES_V7X), 1)
    block_onehot = jnp.where(pos // MOBA_BLOCK == lane, 1.0, 0.0).astype(BF16)
    n_pairs = q.shape[1] // LANES_V7X
    q_t = q.T.astype(BF16)
    v_t = v.T.astype(BF16)
    for p in range(n_pairs):
        cols = slice(p * LANES_V7X, (p + 1) * LANES_V7X)
        q_ref[0, p] = q_t[cols]
        k_ref[0, p, :, :LANES_V7X] = k[:, cols].astype(BF16)
        k_ref[0, p, :, LANES_V7X:] = block_onehot
    n_heads = q.shape[1] // head_dim
    for h in range(n_heads):
        v_ref[0, h, :head_dim] = v_t[h * head_dim:(h + 1) * head_dim]
        v_ref[0, h, head_dim:] = jnp.ones((v_ref.shape[2] - head_dim, tm), BF16)
    kbar_ref[0] = jnp.mean(k.reshape(tm // MOBA_BLOCK, MOBA_BLOCK, k.shape[1]), axis=1)


def _qkv_proj(x, w_qkv, layer, *, batch, seq, tm, tn):
    t, d = x.shape
    n_pairs = d // LANES_V7X
    pairs_per_step = tn // LANES_V7X
    tiles_per_seq = seq // tm
    n_col = d // tn
    head_dim = d // N_HEADS
    heads_per_step = tn // head_dim
    v_rows = head_dim + BF16_SUBLANES_V7X

    def row_major_map(i, j):
        return (i // tiles_per_seq, j, i % tiles_per_seq, 0)

    def feature_major_map(i, j):
        return (i // tiles_per_seq, j, 0, i % tiles_per_seq)

    def w_cols(which):
        return pl.BlockSpec((None, d, tn), lambda i, j: (layer, 0, which * n_col + j))

    return pl.pallas_call(
        functools.partial(_qkv_kernel, head_dim=head_dim, tiles_per_seq=tiles_per_seq),
        grid=(t // tm, n_col),
        in_specs=[pl.BlockSpec((tm, d), lambda i, j: (i, 0)), w_cols(0), w_cols(1), w_cols(2)],
        out_specs=[
            pl.BlockSpec((1, pairs_per_step, LANES_V7X, tm), feature_major_map),
            pl.BlockSpec((1, pairs_per_step, tm, 2 * LANES_V7X), row_major_map),
            pl.BlockSpec((1, heads_per_step, v_rows, tm), feature_major_map),
            pl.BlockSpec((1, tm // MOBA_BLOCK, tn),
                         lambda i, j: (i // tiles_per_seq, i % tiles_per_seq, j)),
        ],
        out_shape=[
            jax.ShapeDtypeStruct((batch, n_pairs, LANES_V7X, seq), BF16),
            jax.ShapeDtypeStruct((batch, n_pairs, seq, 2 * LANES_V7X), BF16),
            jax.ShapeDtypeStruct((batch, N_HEADS, v_rows, seq), BF16),
            jax.ShapeDtypeStruct((batch, seq // MOBA_BLOCK, d), F32),
        ],
        compiler_params=_params(2),
        name="moba_qkv_proj",
    )(x, w_qkv, w_qkv, w_qkv)


def _moba_attn_kernel(q_ref, k_ref, v_ref, kbar_ref, tab_ref, o_ref,
                      s_ref, cmax_ref, acc_ref, *, head_dim):
    tile = pl.program_id(2)
    blk_len = MOBA_BLOCK
    q_t = q_ref[0, 0]
    tq = q_t.shape[1]
    n_blocks = kbar_ref.shape[1]
    feat_row = lax.broadcasted_iota(jnp.int32, q_t.shape, 0)
    kbar = kbar_ref[0].astype(BF16)
    blk = lax.broadcasted_iota(jnp.int32, (n_blocks, tq), 0)
    query = lax.broadcasted_iota(jnp.int32, (n_blocks, tq), 1)
    cur = 2 * tile + jnp.where(query >= blk_len, 1, 0)

    q_aug = []
    for h in range(2):
        in_head = (feat_row >= head_dim) if h else (feat_row < head_dim)
        qm = jnp.where(in_head, q_t, jnp.zeros_like(q_t))
        gate = jnp.dot(kbar, qm, preferred_element_type=F32)
        gate = jnp.where(blk < cur, gate, -jnp.inf)
        rank = jnp.zeros(gate.shape, F32)
        for m in range(n_blocks):
            gm = gate[m:m + 1, :]
            tie = jnp.where(blk > m, 1.0, 0.0)
            rank = rank + jnp.where(gm > gate, 1.0, jnp.where(gm == gate, tie, 0.0))
        feat = jnp.where(blk < cur,
                         jnp.where(rank < MOBA_TOPK, 0.0, MASK_NEG),
                         jnp.where(blk == cur, 0.0, MASK_NEG))
        feat = jnp.concatenate(
            [feat, jnp.zeros((LANES_V7X - n_blocks, tq), F32)], axis=0)
        q_aug.append(jnp.concatenate([qm, feat.astype(BF16)], axis=0))
    q_aug = jnp.concatenate(q_aug, axis=1)
    width = q_aug.shape[1]
    pair_len = 2 * blk_len

    def scores(pair):
        start = pl.multiple_of(pair * pair_len, pair_len)
        return jnp.dot(k_ref[0, 0, pl.ds(start, pair_len), :], q_aug,
                       preferred_element_type=F32)

    def keep(pair, s):
        s_ref[pair] = s
        return jnp.max(s.reshape(pair_len // SUBLANES_V7X, SUBLANES_V7X, width), axis=0)

    def own_pair():
        return keep(tile, scores(tile) + tab_ref[0, blk_len:])

    @pl.when(tile == 0)
    def _():
        cmax_ref[...] = own_pair()

    @pl.when(tile > 0)
    def _():
        s = scores(tile - 1)
        s = jnp.concatenate([s[:blk_len], s[blk_len:] + tab_ref[0, :blk_len]], axis=0)
        cmax_ref[...] = jnp.maximum(own_pair(), keep(tile - 1, s))

    n_far = jnp.maximum(tile - 1, 0)
    cmax_ref[...] = lax.fori_loop(
        0, n_far // 2,
        lambda i, c: jnp.maximum(c, jnp.maximum(keep(2 * i, scores(2 * i)),
                                                keep(2 * i + 1, scores(2 * i + 1)))),
        cmax_ref[...])

    @pl.when(n_far % 2 == 1)
    def _():
        cmax_ref[...] = jnp.maximum(cmax_ref[...], keep(n_far - 1, scores(n_far - 1)))

    col_m = jnp.max(cmax_ref[...], axis=0, keepdims=True)

    acc_ref[...] = jnp.zeros_like(acc_ref)

    def accumulate(first, count):
        s = s_ref[pl.ds(first, count)].reshape(count * pair_len, width)
        p = jnp.exp((s - col_m).astype(BF16))
        start = pl.multiple_of(first * pair_len, pair_len)
        for h in range(2):
            acc_ref[h] += jnp.dot(v_ref[0, h, :, pl.ds(start, count * pair_len)],
                                  p[:, h * tq:(h + 1) * tq], preferred_element_type=F32)

    def accumulate_two(i, carry):
        accumulate(2 * i, 2)
        return carry

    lax.fori_loop(0, (tile + 1) // 2, accumulate_two, 0)

    @pl.when(tile % 2 == 0)
    def _():
        accumulate(tile, 1)

    heads = []
    for h in range(2):
        acc = acc_ref[h]
        heads.append(acc[:head_dim] / acc[head_dim:head_dim + 1])
    o_ref[0] = jnp.concatenate(heads, axis=0).T.astype(o_ref.dtype)


def _rel_bucket(dist):
    n = jnp.maximum(dist, 0)
    max_exact = REL_BUCKETS // 2
    nf = jnp.maximum(n, 1).astype(F32)
    large = max_exact + (jnp.log(nf / max_exact) / math.log(REL_MAX_DIST / max_exact)
                         * (REL_BUCKETS - max_exact)).astype(jnp.int32)
    large = jnp.minimum(large, REL_BUCKETS - 1)
    return jnp.where(n < max_exact, n, large)


def _bias_tables(rel_bias):
    blk = MOBA_BLOCK
    n_keys, n_q = 3 * blk, 2 * blk
    n_heads = rel_bias.shape[1]
    bias_h = rel_bias.T
    d_min = blk - (n_keys - 1)
    dist = jnp.arange(d_min, blk + n_q)
    by_dist = bias_h[:, _rel_bucket(dist)] - bias_h[:, REL_BUCKETS - 1:]
    by_dist = jnp.where(dist[None] < 0, MASK_NEG, by_dist)
    period = by_dist.shape[1] + 1
    assert period % LANES_V7X == 0 and (period - n_q) % LANES_V7X == 0
    rows = jnp.pad(by_dist, ((0, 0), (0, 1)))[:, None, :]
    shift = (period - n_q) - (blk - d_min)

    def expand(rows_ref, out_ref):
        for h in range(2):
            wide = jnp.broadcast_to(rows_ref[h], (n_keys, period))
            rolled = pltpu.roll(wide, shift, 1, stride=1, stride_axis=0)
            out_ref[0, :, h * n_q:(h + 1) * n_q] = rolled[:, period - n_q:]

    return pl.pallas_call(
        expand,
        grid=(n_heads // 2,),
        in_specs=[pl.BlockSpec((2, 1, period), lambda p: (p, 0, 0))],
        out_specs=pl.BlockSpec((1, n_keys, 2 * n_q), lambda p: (p, 0, 0)),
        out_shape=jax.ShapeDtypeStruct((n_heads // 2, n_keys, 2 * n_q), F32),
        compiler_params=_params(1),
        name="rel_bias_tables",
    )(rows)


def _moba_attention(q_t, kaug, v_t, kbar, tables, *, batch, seq):
    n_pairs = q_t.shape[1]
    d = n_pairs * LANES_V7X
    tq = 2 * MOBA_BLOCK
    n_blocks = seq // MOBA_BLOCK
    v_rows = v_t.shape[2]
    return pl.pallas_call(
        functools.partial(_moba_attn_kernel, head_dim=d // N_HEADS),
        grid=(n_pairs, batch, seq // tq),
        in_specs=[
            pl.BlockSpec((1, 1, LANES_V7X, tq), lambda p, b, t: (b, p, 0, t)),
            pl.BlockSpec((1, 1, seq, 2 * LANES_V7X), lambda p, b, t: (b, p, 0, 0)),
            pl.BlockSpec((1, 2, v_rows, seq), lambda p, b, t: (b, p, 0, 0)),
            pl.BlockSpec((1, n_blocks, LANES_V7X), lambda p, b, t: (b, 0, p)),
            pl.BlockSpec((1,) + tables.shape[1:], lambda p, b, t: (p, 0, 0)),
        ],
        out_specs=pl.BlockSpec((1, tq, LANES_V7X), lambda p, b, t: (b, t, p)),
        out_shape=jax.ShapeDtypeStruct((batch, seq, d), BF16),
        scratch_shapes=[
            pltpu.VMEM((seq // tq, tq, 2 * tq), F32),
            pltpu.VMEM((SUBLANES_V7X, 2 * tq), F32),
            pltpu.VMEM((2, v_rows, tq), F32),
        ],
        compiler_params=_params(3),
        name="moba_attention",
    )(q_t, kaug, v_t, kbar, tables)


def _tiles(seq):
    assert seq % (SUBLANES_V7X * MOBA_BLOCK) == 0, "qkv tile must hold 8 whole key blocks"
    return dict(tm_mix=min(seq, 512), tm_ffn=min(seq, 512), tm_qkv=SUBLANES_V7X * MOBA_BLOCK,
                tn_qkv=2 * LANES_V7X, fc_ffn=2 * LANES_V7X)


def kernel(x, ln_mix_g, ln_mix_b, ln_ffn_g, ln_ffn_b, a_w_in, a_ln_g, a_ln_b, a_w_s, a_b_s,
           a_w_out, b_w_qkv, b_w_o, rel_bias, f_w_up, f_conv_w, f_conv_b, f_w_down):
    batch, seq, d = x.shape
    depth = ln_mix_g.shape[0]
    alpha = (2 * depth) ** 0.25
    cfg = _tiles(seq)
    width = a_w_out.shape[1]
    n_groups = a_w_s.shape[1]

    def rows(p):
        return p[:, None, :]

    a_w_in, a_w_out, b_w_qkv, b_w_o, f_w_up, f_w_down = (
        w.astype(BF16) for w in (a_w_in, a_w_out, b_w_qkv, b_w_o, f_w_up, f_w_down))
    ln_mix_g, ln_mix_b, ln_ffn_g, ln_ffn_b, a_ln_g, a_ln_b, f_conv_b = (
        rows(p) for p in (ln_mix_g, ln_mix_b, ln_ffn_g, ln_ffn_b, a_ln_g, a_ln_b, f_conv_b))
    bs_tile = jnp.repeat(a_b_s.transpose(0, 2, 1), width // n_groups, axis=2)
    tables = _bias_tables(rel_bias)

    h = x.reshape(batch * seq, d)
    for i in range(depth):
        j = i // 2
        if i % 2 == 0:
            mix = _gmlp_mixer(h, a_w_in, a_ln_g, a_ln_b, a_w_s, bs_tile, j, tm=cfg["tm_mix"])
            w_proj = a_w_out
        else:
            q_t, kaug, v_t, kbar = _qkv_proj(h, b_w_qkv, j, batch=batch, seq=seq,
                                             tm=cfg["tm_qkv"], tn=cfg["tn_qkv"])
            mix = _moba_attention(q_t, kaug, v_t, kbar, tables, batch=batch, seq=seq)
            mix = mix.reshape(batch * seq, d)
            w_proj = b_w_o
        h = _channel_stage(mix, h, w_proj, j, ln_mix_g, ln_mix_b, f_w_up, f_conv_w, f_conv_b,
                           f_w_down, ln_ffn_g, ln_ffn_b, i, alpha=alpha, tm=cfg["tm_ffn"],
                           fc=cfg["fc_ffn"], seq=seq)
    return h.reshape(batch, seq, d)
```

```python
import functools
import math

import jax
import jax.numpy as jnp
from jax import lax
from jax.experimental import pallas as pl
from jax.experimental.pallas import tpu as pltpu

F32 = jnp.float32
BF16 = jnp.bfloat16

GMLP_CHUNK = 128
N_HEADS = 16
MOBA_BLOCK = 256
MOBA_TOPK = 3
REL_BUCKETS = 32
REL_MAX_DIST = 128
LN_EPS = 1e-5

LANES_V7X = 128
SUBLANES_V7X = 8
BF16_SUBLANES_V7X = 16
VMEM_LIMIT_BYTES_V7X = 56 * 1024 * 1024

MASK_NEG = -1e30


def _layer_norm(x, g, b):
    mu = jnp.mean(x, axis=-1, keepdims=True)
    xc = x - mu
    var = jnp.mean(xc * xc, axis=-1, keepdims=True)
    return xc * lax.rsqrt(var + LN_EPS) * g + b


def _layer_block(stacked, layer):
    tail = stacked.shape[1:]
    return pl.BlockSpec((None,) + tail, lambda *_: (layer,) + (0,) * len(tail),
                        pipeline_mode=pl.Buffered(1))


def _params(n_axes):
    return pltpu.CompilerParams(
        dimension_semantics=("arbitrary",) * n_axes,
        vmem_limit_bytes=VMEM_LIMIT_BYTES_V7X,
    )


def _gmlp_kernel(x_ref, w_in_ref, lng_ref, lnb_ref, ws_ref, bs_ref, o_ref, *, n_groups):
    x = x_ref[...]
    tm = x.shape[0]
    width = o_ref.shape[1]
    gdim = width // n_groups
    z = jnp.dot(x.astype(BF16), w_in_ref[...], preferred_element_type=F32)
    z = jax.nn.gelu(z)
    u = z[:, :width]
    v = _layer_norm(z[:, width:], lng_ref[...], lnb_ref[...]).astype(BF16)
    row = lax.broadcasted_iota(jnp.int32, (GMLP_CHUNK, GMLP_CHUNK), 0)
    col = lax.broadcasted_iota(jnp.int32, (GMLP_CHUNK, GMLP_CHUNK), 1)
    causal = col <= row
    w_masked = [jnp.where(causal, ws_ref[g], 0.0).astype(BF16) for g in range(n_groups)]
    chunks = []
    for c in range(tm // GMLP_CHUNK):
        rows = slice(c * GMLP_CHUNK, (c + 1) * GMLP_CHUNK)
        groups = [
            jnp.dot(w_masked[g], v[rows, g * gdim:(g + 1) * gdim], preferred_element_type=F32)
            for g in range(n_groups)
        ]
        chunks.append(jnp.concatenate(groups, axis=1) + bs_ref[...])
    sv = jnp.concatenate(chunks, axis=0)
    o_ref[...] = (u * sv).astype(o_ref.dtype)


def _gmlp_mixer(x, w_in, ln_g, ln_b, w_s, bs_tile, layer, *, tm):
    t, d = x.shape
    width = w_in.shape[2] // 2
    return pl.pallas_call(
        functools.partial(_gmlp_kernel, n_groups=w_s.shape[1]),
        grid=(t // tm,),
        in_specs=[pl.BlockSpec((tm, d), lambda i: (i, 0))]
        + [_layer_block(a, layer) for a in (w_in, ln_g, ln_b, w_s, bs_tile)],
        out_specs=pl.BlockSpec((tm, width), lambda i: (i, 0)),
        out_shape=jax.ShapeDtypeStruct((t, width), BF16),
        compiler_params=_params(1),
        name="gmlp_mixer",
    )(x, w_in, ln_g, ln_b, w_s, bs_tile)


def _causal_conv(h, prev, cw, cb):
    head = jnp.concatenate([prev, h[:SUBLANES_V7X]], axis=0)

    def shifted(rows):
        top = pltpu.roll(head, rows, 0)[SUBLANES_V7X:]
        return jnp.concatenate([top, pltpu.roll(h, rows, 0)[SUBLANES_V7X:]], axis=0).astype(BF16)

    cw = cw.astype(BF16)
    return (cb.astype(BF16) + cw[0:1] * shifted(2) + cw[1:2] * shifted(1)
            + cw[2:3] * h.astype(BF16))


def _channel_kernel(mix0_ref, x0_ref, mix_ref, x_ref, w_proj_ref, g_mix_ref, b_mix_ref,
                    w_up_ref, cw_ref, cb_ref, w_down_ref, g_ffn_ref, b_ffn_ref, o_ref,
                    xmid_ref, xnext_ref, ysum_ref, act_ref, carry_ref,
                    *, alpha, fc, tiles_per_seq, n_tiles):
    step = pl.program_id(0)
    tm = x_ref.shape[0]
    ffn = w_down_ref.shape[0]

    def mixer_sublayer(mix, x):
        y = jnp.dot(mix, w_proj_ref[...], preferred_element_type=F32)
        return _layer_norm(alpha * x + y, g_mix_ref[...], b_mix_ref[...])

    def final_norm():
        return _layer_norm(ysum_ref[...], g_ffn_ref[...], b_ffn_ref[...])

    @pl.when(step == 0)
    def _():
        xmid_ref[...] = mixer_sublayer(mix0_ref[...], x0_ref[...])
        ysum_ref[...] = jnp.zeros_like(ysum_ref)

    @pl.when(step < n_tiles)
    def _():
        @pl.when(step % tiles_per_seq == 0)
        def _():
            carry_ref[...] = jnp.zeros_like(carry_ref)

        x = xmid_ref[...]
        xb = x.astype(BF16)

        def conv_cols(start):
            cols = slice(start, start + fc)
            h = jnp.dot(xb, w_up_ref[:, cols], preferred_element_type=F32)
            prev = carry_ref[:, cols]
            carry_ref[:, cols] = h[tm - SUBLANES_V7X:]
            return _causal_conv(h, prev, cw_ref[:, cols], cb_ref[:, cols])

        for j in range(ffn // fc):
            gate = conv_cols(j * fc)
            val = conv_cols(ffn + j * fc)
            act_ref[:, j * fc:(j + 1) * fc] = jax.nn.gelu(gate) * val

        o_ref[...] = final_norm()
        xnext_ref[...] = mixer_sublayer(mix_ref[...], x_ref[...])
        y = jnp.dot(act_ref[...], w_down_ref[...], preferred_element_type=F32)
        ysum_ref[...] = alpha * x + y
        xmid_ref[...] = xnext_ref[...]

    @pl.when(step == n_tiles)
    def _():
        o_ref[...] = final_norm()


def _channel_stage(mix, x, w_proj, proj_layer, ln_mix_g, ln_mix_b, w_up, conv_w, conv_b, w_down,
                   ln_ffn_g, ln_ffn_b, layer, *, alpha, tm, fc, seq):
    t, d = x.shape
    ffn = w_down.shape[1]
    n_tiles = t // tm
    width = mix.shape[1]

    def first_tile(cols):
        return pl.BlockSpec((tm, cols), lambda i: (0, 0), pipeline_mode=pl.Buffered(1))

    def next_tile(cols):
        return pl.BlockSpec((tm, cols), lambda i: (jnp.minimum(i + 1, n_tiles - 1), 0))

    return pl.pallas_call(
        functools.partial(_channel_kernel, alpha=alpha, fc=fc, tiles_per_seq=seq // tm,
                          n_tiles=n_tiles),
        grid=(n_tiles + 1,),
        in_specs=[first_tile(width), first_tile(d), next_tile(width), next_tile(d),
                  _layer_block(w_proj, proj_layer)]
        + [_layer_block(a, layer)
           for a in (ln_mix_g, ln_mix_b, w_up, conv_w, conv_b, w_down, ln_ffn_g, ln_ffn_b)],
        out_specs=pl.BlockSpec((tm, d), lambda i: (jnp.maximum(i - 1, 0), 0)),
        out_shape=jax.ShapeDtypeStruct((t, d), F32),
        scratch_shapes=[
            pltpu.VMEM((tm, d), F32),
            pltpu.VMEM((tm, d), F32),
            pltpu.VMEM((tm, d), F32),
            pltpu.VMEM((tm, ffn), BF16),
            pltpu.VMEM((SUBLANES_V7X, 2 * ffn), F32),
        ],
        compiler_params=_params(1),
        name="channel_stage",
    )(mix, x, mix, x, w_proj, ln_mix_g, ln_mix_b, w_up, conv_w, conv_b, w_down,
      ln_ffn_g, ln_ffn_b)


def _qkv_kernel(x_ref, wq_ref, wk_ref, wv_ref, q_ref, k_ref, v_ref, kbar_ref,
                *, head_dim, tiles_per_seq):
    q_scale = head_dim ** -0.5
    xb = x_ref[...].astype(BF16)
    tm = xb.shape[0]
    q = jnp.dot(xb, wq_ref[...], preferred_element_type=F32) * q_scale
    k = jnp.dot(xb, wk_ref[...], preferred_element_type=F32)
    v = jnp.dot(xb, wv_ref[...], preferred_element_type=F32)
    pos = (pl.program_id(0) % tiles_per_seq) * tm + lax.broadcasted_iota(
        jnp.int32, (tm, LANES_V7X), 0)
    lane = lax.broadcasted_iota(jnp.int32, (tm, LANES_V7X), 1)
    block_onehot = jnp.where(pos // MOBA_BLOCK == lane, 1.0, 0.0).astype(BF16)
    n_pairs = q.shape[1] // LANES_V7X
    q_t = q.T.astype(BF16)
    v_t = v.T.astype(BF16)
    for p in range(n_pairs):
        cols = slice(p * LANES_V7X, (p + 1) * LANES_V7X)
        q_ref[0, p] = q_t[cols]
        k_ref[0, p, :, :LANES_V7X] = k[:, cols].astype(BF16)
        k_ref[0, p, :, LANES_V7X:] = block_onehot
    n_heads = q.shape[1] // head_dim
    for h in range(n_heads):
        v_ref[0, h, :head_dim] = v_t[h * head_dim:(h + 1) * head_dim]
        v_ref[0, h, head_dim:] = jnp.ones((v_ref.shape[2] - head_dim, tm), BF16)
    kbar_ref[0] = jnp.mean(k.reshape(tm // MOBA_BLOCK, MOBA_BLOCK, k.shape[1]), axis=1)


def _qkv_proj(x, w_qkv, layer, *, batch, seq, tm, tn):
    t, d = x.shape
    n_pairs = d // LANES_V7X
    pairs_per_step = tn // LANES_V7X
    tiles_per_seq = seq // tm
    n_col = d // tn
    head_dim = d // N_HEADS
    heads_per_step = tn // head_dim
    v_rows = head_dim + BF16_SUBLANES_V7X

    def row_major_map(i, j):
        return (i // tiles_per_seq, j, i % tiles_per_seq, 0)

    def feature_major_map(i, j):
        return (i // tiles_per_seq, j, 0, i % tiles_per_seq)

    def w_cols(which):
        return pl.BlockSpec((None, d, tn), lambda i, j: (layer, 0, which * n_col + j))

    return pl.pallas_call(
        functools.partial(_qkv_kernel, head_dim=head_dim, tiles_per_seq=tiles_per_seq),
        grid=(t // tm, n_col),
        in_specs=[pl.BlockSpec((tm, d), lambda i, j: (i, 0)), w_cols(0), w_cols(1), w_cols(2)],
        out_specs=[
            pl.BlockSpec((1, pairs_per_step, LANES_V7X, tm), feature_major_map),
            pl.BlockSpec((1, pairs_per_step, tm, 2 * LANES_V7X), row_major_map),
            pl.BlockSpec((1, heads_per_step, v_rows, tm), feature_major_map),
            pl.BlockSpec((1, tm // MOBA_BLOCK, tn),
                         lambda i, j: (i // tiles_per_seq, i % tiles_per_seq, j)),
        ],
        out_shape=[
            jax.ShapeDtypeStruct((batch, n_pairs, LANES_V7X, seq), BF16),
            jax.ShapeDtypeStruct((batch, n_pairs, seq, 2 * LANES_V7X), BF16),
            jax.ShapeDtypeStruct((batch, N_HEADS, v_rows, seq), BF16),
            jax.ShapeDtypeStruct((batch, seq // MOBA_BLOCK, d), F32),
        ],
        compiler_params=_params(2),
        name="moba_qkv_proj",
    )(x, w_qkv, w_qkv, w_qkv)


def _moba_attn_kernel(q_ref, k_ref, v_ref, kbar_ref, tab_ref, o_ref,
                      s_ref, cmax_ref, acc_ref, *, head_dim):
    def one_tile(tile, carry):
        _attend_tile(tile, q_ref, k_ref, v_ref, kbar_ref, tab_ref, o_ref,
                     s_ref, cmax_ref, acc_ref, head_dim=head_dim)
        return carry

    lax.fori_loop(0, s_ref.shape[0], one_tile, 0)


def _attend_tile(tile, q_ref, k_ref, v_ref, kbar_ref, tab_ref, o_ref,
                 s_ref, cmax_ref, acc_ref, *, head_dim):
    blk_len = MOBA_BLOCK
    tq = 2 * blk_len
    q_start = pl.multiple_of(tile * tq, tq)
    q_t = q_ref[0, 0, :, pl.ds(q_start, tq)]
    n_blocks = kbar_ref.shape[1]
    feat_row = lax.broadcasted_iota(jnp.int32, q_t.shape, 0)
    kbar = kbar_ref[0].astype(BF16)
    blk = lax.broadcasted_iota(jnp.int32, (n_blocks, tq), 0)
    query = lax.broadcasted_iota(jnp.int32, (n_blocks, tq), 1)
    cur = 2 * tile + jnp.where(query >= blk_len, 1, 0)

    q_aug = []
    for h in range(2):
        in_head = (feat_row >= head_dim) if h else (feat_row < head_dim)
        qm = jnp.where(in_head, q_t, jnp.zeros_like(q_t))
        gate = jnp.dot(kbar, qm, preferred_element_type=F32)
        gate = jnp.where(blk < cur, gate, -jnp.inf)
        rank = jnp.zeros(gate.shape, F32)
        for m in range(n_blocks):
            gm = gate[m:m + 1, :]
            tie = jnp.where(blk > m, 1.0, 0.0)
            rank = rank + jnp.where(gm > gate, 1.0, jnp.where(gm == gate, tie, 0.0))
        feat = jnp.where(blk < cur,
                         jnp.where(rank < MOBA_TOPK, 0.0, MASK_NEG),
                         jnp.where(blk == cur, 0.0, MASK_NEG))
        feat = jnp.concatenate(
            [feat, jnp.zeros((LANES_V7X - n_blocks, tq), F32)], axis=0)
        q_aug.append(jnp.concatenate([qm, feat.astype(BF16)], axis=0))
    q_aug = jnp.concatenate(q_aug, axis=1)
    width = q_aug.shape[1]
    pair_len = 2 * blk_len

    def scores(pair):
        start = pl.multiple_of(pair * pair_len, pair_len)
        return jnp.dot(k_ref[0, 0, pl.ds(start, pair_len), :], q_aug,
                       preferred_element_type=F32)

    def keep(pair, s):
        s_ref[pair] = s
        return jnp.max(s.reshape(pair_len // SUBLANES_V7X, SUBLANES_V7X, width), axis=0)

    def own_pair():
        return keep(tile, scores(tile) + tab_ref[0, blk_len:])

    @pl.when(tile == 0)
    def _():
        cmax_ref[...] = own_pair()

    @pl.when(tile > 0)
    def _():
        s = scores(tile - 1)
        s = jnp.concatenate([s[:blk_len], s[blk_len:] + tab_ref[0, :blk_len]], axis=0)
        cmax_ref[...] = jnp.maximum(own_pair(), keep(tile - 1, s))

    n_far = jnp.maximum(tile - 1, 0)
    cmax_ref[...] = lax.fori_loop(
        0, n_far // 2,
        lambda i, c: jnp.maximum(c, jnp.maximum(keep(2 * i, scores(2 * i)),
                                                keep(2 * i + 1, scores(2 * i + 1)))),
        cmax_ref[...])

    @pl.when(n_far % 2 == 1)
    def _():
        cmax_ref[...] = jnp.maximum(cmax_ref[...], keep(n_far - 1, scores(n_far - 1)))

    col_m = jnp.max(cmax_ref[...], axis=0, keepdims=True)

    acc_ref[...] = jnp.zeros_like(acc_ref)

    def accumulate(first, count):
        s = s_ref[pl.ds(first, count)].reshape(count * pair_len, width)
        p = jnp.exp((s - col_m).astype(BF16))
        start = pl.multiple_of(first * pair_len, pair_len)
        for h in range(2):
            acc_ref[h] += jnp.dot(v_ref[0, h, :, pl.ds(start, count * pair_len)],
                                  p[:, h * tq:(h + 1) * tq], preferred_element_type=F32)

    def accumulate_two(i, carry):
        accumulate(2 * i, 2)
        return carry

    lax.fori_loop(0, (tile + 1) // 2, accumulate_two, 0)

    @pl.when(tile % 2 == 0)
    def _():
        accumulate(tile, 1)

    heads = []
    for h in range(2):
        acc = acc_ref[h]
        heads.append(acc[:head_dim] / acc[head_dim:head_dim + 1])
    o_ref[0, pl.ds(q_start, tq), :] = jnp.concatenate(heads, axis=0).T.astype(o_ref.dtype)


def _rel_bucket(dist):
    n = jnp.maximum(dist, 0)
    max_exact = REL_BUCKETS // 2
    nf = jnp.maximum(n, 1).astype(F32)
    large = max_exact + (jnp.log(nf / max_exact) / math.log(REL_MAX_DIST / max_exact)
                         * (REL_BUCKETS - max_exact)).astype(jnp.int32)
    large = jnp.minimum(large, REL_BUCKETS - 1)
    return jnp.where(n < max_exact, n, large)


def _bias_tables(rel_bias):
    blk = MOBA_BLOCK
    n_keys, n_q = 3 * blk, 2 * blk
    n_heads = rel_bias.shape[1]
    bias_h = rel_bias.T
    d_min = blk - (n_keys - 1)
    dist = jnp.arange(d_min, blk + n_q)
    by_dist = bias_h[:, _rel_bucket(dist)] - bias_h[:, REL_BUCKETS - 1:]
    by_dist = jnp.where(dist[None] < 0, MASK_NEG, by_dist)
    period = by_dist.shape[1] + 1
    assert period % LANES_V7X == 0 and (period - n_q) % LANES_V7X == 0
    rows = jnp.pad(by_dist, ((0, 0), (0, 1)))[:, None, :]
    shift = (period - n_q) - (blk - d_min)

    def expand(rows_ref, out_ref):
        for h in range(2):
            wide = jnp.broadcast_to(rows_ref[h], (n_keys, period))
            rolled = pltpu.roll(wide, shift, 1, stride=1, stride_axis=0)
            out_ref[0, :, h * n_q:(h + 1) * n_q] = rolled[:, period - n_q:]

    return pl.pallas_call(
        expand,
        grid=(n_heads // 2,),
        in_specs=[pl.BlockSpec((2, 1, period), lambda p: (p, 0, 0))],
        out_specs=pl.BlockSpec((1, n_keys, 2 * n_q), lambda p: (p, 0, 0)),
        out_shape=jax.ShapeDtypeStruct((n_heads // 2, n_keys, 2 * n_q), F32),
        compiler_params=_params(1),
        name="rel_bias_tables",
    )(rows)


def _moba_attention(q_t, kaug, v_t, kbar, tables, *, batch, seq):
    n_pairs = q_t.shape[1]
    d = n_pairs * LANES_V7X
    tq = 2 * MOBA_BLOCK
    n_blocks = seq // MOBA_BLOCK
    v_rows = v_t.shape[2]
    return pl.pallas_call(
        functools.partial(_moba_attn_kernel, head_dim=d // N_HEADS),
        grid=(n_pairs, batch),
        in_specs=[
            pl.BlockSpec((1, 1, LANES_V7X, seq), lambda p, b: (b, p, 0, 0)),
            pl.BlockSpec((1, 1, seq, 2 * LANES_V7X), lambda p, b: (b, p, 0, 0)),
            pl.BlockSpec((1, 2, v_rows, seq), lambda p, b: (b, p, 0, 0)),
            pl.BlockSpec((1, n_blocks, LANES_V7X), lambda p, b: (b, 0, p)),
            pl.BlockSpec((1,) + tables.shape[1:], lambda p, b: (p, 0, 0)),
        ],
        out_specs=pl.BlockSpec((1, seq, LANES_V7X), lambda p, b: (b, 0, p)),
        out_shape=jax.ShapeDtypeStruct((batch, seq, d), BF16),
        scratch_shapes=[
            pltpu.VMEM((seq // tq, tq, 2 * tq), F32),
            pltpu.VMEM((SUBLANES_V7X, 2 * tq), F32),
            pltpu.VMEM((2, v_rows, tq), F32),
        ],
        compiler_params=_params(2),
        name="moba_attention",
    )(q_t, kaug, v_t, kbar, tables)


def _tiles(seq):
    assert seq % (SUBLANES_V7X * MOBA_BLOCK) == 0, "qkv tile must hold 8 whole key blocks"
    return dict(tm_mix=min(seq, 512), tm_ffn=min(seq, 512), tm_qkv=SUBLANES_V7X * MOBA_BLOCK,
                tn_qkv=2 * LANES_V7X, fc_ffn=2 * LANES_V7X)


def kernel(x, ln_mix_g, ln_mix_b, ln_ffn_g, ln_ffn_b, a_w_in, a_ln_g, a_ln_b, a_w_s, a_b_s,
           a_w_out, b_w_qkv, b_w_o, rel_bias, f_w_up, f_conv_w, f_conv_b, f_w_down):
    batch, seq, d = x.shape
    depth = ln_mix_g.shape[0]
    alpha = (2 * depth) ** 0.25
    cfg = _tiles(seq)
    width = a_w_out.shape[1]
    n_groups = a_w_s.shape[1]

    def rows(p):
        return p[:, None, :]

    a_w_in, a_w_out, b_w_qkv, b_w_o, f_w_up, f_w_down = (
        w.astype(BF16) for w in (a_w_in, a_w_out, b_w_qkv, b_w_o, f_w_up, f_w_down))
    ln_mix_g, ln_mix_b, ln_ffn_g, ln_ffn_b, a_ln_g, a_ln_b, f_conv_b = (
        rows(p) for p in (ln_mix_g, ln_mix_b, ln_ffn_g, ln_ffn_b, a_ln_g, a_ln_b, f_conv_b))
    bs_tile = jnp.repeat(a_b_s.transpose(0, 2, 1), width // n_groups, axis=2)
    tables = _bias_tables(rel_bias)

    h = x.reshape(batch * seq, d)
    for i in range(depth):
        j = i // 2
        if i % 2 == 0:
            mix = _gmlp_mixer(h, a_w_in, a_ln_g, a_ln_b, a_w_s, bs_tile, j, tm=cfg["tm_mix"])
            w_proj = a_w_out
        else:
            q_t, kaug, v_t, kbar = _qkv_proj(h, b_w_qkv, j, batch=batch, seq=seq,
                                             tm=cfg["tm_qkv"], tn=cfg["tn_qkv"])
            mix = _moba_attention(q_t, kaug, v_t, kbar, tables, batch=batch, seq=seq)
            mix = mix.reshape(batch * seq, d)
            w_proj = b_w_o
        h = _channel_stage(mix, h, w_proj, j, ln_mix_g, ln_mix_b, f_w_up, f_conv_w, f_conv_b,
                           f_w_down, ln_ffn_g, ln_ffn_b, i, alpha=alpha, tm=cfg["tm_ffn"],
                           fc=cfg["fc_ffn"], seq=seq)
    return h.reshape(batch, seq, d)
```

```python
import functools
import math

import jax
import jax.numpy as jnp
from jax import lax
from jax.experimental import pallas as pl
from jax.experimental.pallas import tpu as pltpu

F32 = jnp.float32
BF16 = jnp.bfloat16

GMLP_CHUNK = 128
N_HEADS = 16
MOBA_BLOCK = 256
MOBA_TOPK = 3
REL_BUCKETS = 32
REL_MAX_DIST = 128
LN_EPS = 1e-5

LANES_V7X = 128
SUBLANES_V7X = 8
BF16_SUBLANES_V7X = 16
VMEM_LIMIT_BYTES_V7X = 56 * 1024 * 1024

MASK_NEG = -1e30
LOG2_E = math.log2(math.e)


def _layer_norm(x, g, b):
    mu = jnp.mean(x, axis=-1, keepdims=True)
    xc = x - mu
    var = jnp.mean(xc * xc, axis=-1, keepdims=True)
    return xc * lax.rsqrt(var + LN_EPS) * g + b


def _layer_block(stacked, layer):
    tail = stacked.shape[1:]
    return pl.BlockSpec((None,) + tail, lambda *_: (layer,) + (0,) * len(tail),
                        pipeline_mode=pl.Buffered(1))


def _params(n_axes):
    return pltpu.CompilerParams(
        dimension_semantics=("arbitrary",) * n_axes,
        vmem_limit_bytes=VMEM_LIMIT_BYTES_V7X,
    )


def _gmlp_kernel(x_ref, w_in_ref, lng_ref, lnb_ref, ws_ref, bs_ref, o_ref, *, n_groups):
    x = x_ref[...]
    tm = x.shape[0]
    width = o_ref.shape[1]
    gdim = width // n_groups
    z = jnp.dot(x.astype(BF16), w_in_ref[...], preferred_element_type=F32)
    z = jax.nn.gelu(z)
    u = z[:, :width]
    v = _layer_norm(z[:, width:], lng_ref[...], lnb_ref[...]).astype(BF16)
    row = lax.broadcasted_iota(jnp.int32, (GMLP_CHUNK, GMLP_CHUNK), 0)
    col = lax.broadcasted_iota(jnp.int32, (GMLP_CHUNK, GMLP_CHUNK), 1)
    causal = col <= row
    w_masked = [jnp.where(causal, ws_ref[g], 0.0).astype(BF16) for g in range(n_groups)]
    chunks = []
    for c in range(tm // GMLP_CHUNK):
        rows = slice(c * GMLP_CHUNK, (c + 1) * GMLP_CHUNK)
        groups = [
            jnp.dot(w_masked[g], v[rows, g * gdim:(g + 1) * gdim], preferred_element_type=F32)
            for g in range(n_groups)
        ]
        chunks.append(jnp.concatenate(groups, axis=1) + bs_ref[...])
    sv = jnp.concatenate(chunks, axis=0)
    o_ref[...] = (u * sv).astype(o_ref.dtype)


def _gmlp_mixer(x, w_in, ln_g, ln_b, w_s, bs_tile, layer, *, tm):
    t, d = x.shape
    width = w_in.shape[2] // 2
    return pl.pallas_call(
        functools.partial(_gmlp_kernel, n_groups=w_s.shape[1]),
        grid=(t // tm,),
        in_specs=[pl.BlockSpec((tm, d), lambda i: (i, 0))]
        + [_layer_block(a, layer) for a in (w_in, ln_g, ln_b, w_s, bs_tile)],
        out_specs=pl.BlockSpec((tm, width), lambda i: (i, 0)),
        out_shape=jax.ShapeDtypeStruct((t, width), BF16),
        compiler_params=_params(1),
        name="gmlp_mixer",
    )(x, w_in, ln_g, ln_b, w_s, bs_tile)


def _causal_conv(h, prev, cw, cb):
    head = jnp.concatenate([prev, h[:SUBLANES_V7X]], axis=0)

    def shifted(rows):
        top = pltpu.roll(head, rows, 0)[SUBLANES_V7X:]
        return jnp.concatenate([top, pltpu.roll(h, rows, 0)[SUBLANES_V7X:]], axis=0).astype(BF16)

    cw = cw.astype(BF16)
    return (cb.astype(BF16) + cw[0:1] * shifted(2) + cw[1:2] * shifted(1)
            + cw[2:3] * h.astype(BF16))


def _channel_kernel(mix0_ref, x0_ref, mix_ref, x_ref, w_proj_ref, g_mix_ref, b_mix_ref,
                    w_up_ref, cw_ref, cb_ref, w_down_ref, g_ffn_ref, b_ffn_ref, o_ref,
                    xmid_ref, xnext_ref, ysum_ref, act_ref, carry_ref,
                    *, alpha, fc, tiles_per_seq, n_tiles):
    step = pl.program_id(0)
    tm = x_ref.shape[0]
    ffn = w_down_ref.shape[0]

    def mixer_sublayer(mix, x):
        y = jnp.dot(mix, w_proj_ref[...], preferred_element_type=F32)
        return _layer_norm(alpha * x + y, g_mix_ref[...], b_mix_ref[...])

    def final_norm():
        return _layer_norm(ysum_ref[...], g_ffn_ref[...], b_ffn_ref[...])

    @pl.when(step == 0)
    def _():
        xmid_ref[...] = mixer_sublayer(mix0_ref[...], x0_ref[...])
        ysum_ref[...] = jnp.zeros_like(ysum_ref)

    @pl.when(step < n_tiles)
    def _():
        @pl.when(step % tiles_per_seq == 0)
        def _():
            carry_ref[...] = jnp.zeros_like(carry_ref)

        x = xmid_ref[...]
        xb = x.astype(BF16)

        def up_proj(j):
            return tuple(
                jnp.dot(xb, w_up_ref[:, start:start + fc], preferred_element_type=F32)
                for start in (j * fc, ffn + j * fc))

        def conv_cols(h, start):
            cols = slice(start, start + fc)
            prev = carry_ref[:, cols]
            carry_ref[:, cols] = h[tm - SUBLANES_V7X:]
            return _causal_conv(h, prev, cw_ref[:, cols], cb_ref[:, cols])

        n_chunks = ffn // fc
        h_pair = up_proj(0)
        for j in range(n_chunks):
            h_gate, h_val = h_pair
            if j + 1 < n_chunks:
                h_pair = up_proj(j + 1)
            gate = conv_cols(h_gate, j * fc)
            val = conv_cols(h_val, ffn + j * fc)
            act_ref[:, j * fc:(j + 1) * fc] = jax.nn.gelu(gate) * val

        o_ref[...] = final_norm()
        xnext_ref[...] = mixer_sublayer(mix_ref[...], x_ref[...])
        y = jnp.dot(act_ref[...], w_down_ref[...], preferred_element_type=F32)
        ysum_ref[...] = alpha * x + y
        xmid_ref[...] = xnext_ref[...]

    @pl.when(step == n_tiles)
    def _():
        o_ref[...] = final_norm()


def _channel_stage(mix, x, w_proj, proj_layer, ln_mix_g, ln_mix_b, w_up, conv_w, conv_b, w_down,
                   ln_ffn_g, ln_ffn_b, layer, *, alpha, tm, fc, seq):
    t, d = x.shape
    ffn = w_down.shape[1]
    n_tiles = t // tm
    width = mix.shape[1]

    def first_tile(cols):
        return pl.BlockSpec((tm, cols), lambda i: (0, 0), pipeline_mode=pl.Buffered(1))

    def next_tile(cols):
        return pl.BlockSpec((tm, cols), lambda i: (jnp.minimum(i + 1, n_tiles - 1), 0))

    return pl.pallas_call(
        functools.partial(_channel_kernel, alpha=alpha, fc=fc, tiles_per_seq=seq // tm,
                          n_tiles=n_tiles),
        grid=(n_tiles + 1,),
        in_specs=[first_tile(width), first_tile(d), next_tile(width), next_tile(d),
                  _layer_block(w_proj, proj_layer)]
        + [_layer_block(a, layer)
           for a in (ln_mix_g, ln_mix_b, w_up, conv_w, conv_b, w_down, ln_ffn_g, ln_ffn_b)],
        out_specs=pl.BlockSpec((tm, d), lambda i: (jnp.maximum(i - 1, 0), 0)),
        out_shape=jax.ShapeDtypeStruct((t, d), F32),
        scratch_shapes=[
            pltpu.VMEM((tm, d), F32),
            pltpu.VMEM((tm, d), F32),
            pltpu.VMEM((tm, d), F32),
            pltpu.VMEM((tm, ffn), BF16),
            pltpu.VMEM((SUBLANES_V7X, 2 * ffn), F32),
        ],
        compiler_params=_params(1),
        name="channel_stage",
    )(mix, x, mix, x, w_proj, ln_mix_g, ln_mix_b, w_up, conv_w, conv_b, w_down,
      ln_ffn_g, ln_ffn_b)


def _qkv_kernel(x_ref, wq_ref, wk_ref, wv_ref, q_ref, k_ref, v_ref, kbar_ref,
                *, head_dim, tiles_per_seq):
    q_scale = head_dim ** -0.5 * LOG2_E
    tm = x_ref.shape[0]
    tn = wq_ref.shape[1]
    n_pairs = tn // LANES_V7X
    n_heads = tn // head_dim
    tile_pos = (pl.program_id(0) % tiles_per_seq) * tm
    sub = 2 * MOBA_BLOCK
    lane = lax.broadcasted_iota(jnp.int32, (MOBA_BLOCK, LANES_V7X), 1)

    def project(r):
        xb = x_ref[r * sub:(r + 1) * sub, :].astype(BF16)
        return tuple(jnp.dot(xb, w_ref[...], preferred_element_type=F32)
                     for w_ref in (wq_ref, wk_ref, wv_ref))

    kbar = []
    n_sub = tm // sub
    projected = project(0)
    for r in range(n_sub):
        rows = slice(r * sub, (r + 1) * sub)
        q, k, v = projected
        if r + 1 < n_sub:
            projected = project(r + 1)
        q = q * q_scale
        first_block = (tile_pos + r * sub) // MOBA_BLOCK
        block_onehot = jnp.concatenate(
            [jnp.where(lane == first_block + b, 1.0, 0.0).astype(BF16)
             for b in range(sub // MOBA_BLOCK)], axis=0)
        q_t = q.T.astype(BF16)
        v_t = v.T.astype(BF16)
        for p in range(n_pairs):
            cols = slice(p * LANES_V7X, (p + 1) * LANES_V7X)
            q_ref[0, p, :, rows] = q_t[cols]
            k_ref[0, p, rows, :LANES_V7X] = k[:, cols].astype(BF16)
            k_ref[0, p, rows, LANES_V7X:] = block_onehot
        for h in range(n_heads):
            v_ref[0, h, :head_dim, rows] = v_t[h * head_dim:(h + 1) * head_dim]
            v_ref[0, h, head_dim:, rows] = jnp.ones((v_ref.shape[2] - head_dim, sub), BF16)
        kbar.append(jnp.mean(k.reshape(sub // MOBA_BLOCK, MOBA_BLOCK, tn), axis=1))
    kbar_ref[0] = jnp.concatenate(kbar, axis=0)


def _qkv_proj(x, w_qkv, layer, *, batch, seq, tm, tn):
    t, d = x.shape
    n_pairs = d // LANES_V7X
    pairs_per_step = tn // LANES_V7X
    tiles_per_seq = seq // tm
    n_col = d // tn
    head_dim = d // N_HEADS
    heads_per_step = tn // head_dim
    v_rows = head_dim + BF16_SUBLANES_V7X

    def row_major_map(i, j):
        return (i // tiles_per_seq, j, i % tiles_per_seq, 0)

    def feature_major_map(i, j):
        return (i // tiles_per_seq, j, 0, i % tiles_per_seq)

    def w_cols(which):
        return pl.BlockSpec((None, d, tn), lambda i, j: (layer, 0, which * n_col + j))

    return pl.pallas_call(
        functools.partial(_qkv_kernel, head_dim=head_dim, tiles_per_seq=tiles_per_seq),
        grid=(t // tm, n_col),
        in_specs=[pl.BlockSpec((tm, d), lambda i, j: (i, 0)), w_cols(0), w_cols(1), w_cols(2)],
        out_specs=[
            pl.BlockSpec((1, pairs_per_step, LANES_V7X, tm), feature_major_map),
            pl.BlockSpec((1, pairs_per_step, tm, 2 * LANES_V7X), row_major_map),
            pl.BlockSpec((1, heads_per_step, v_rows, tm), feature_major_map),
            pl.BlockSpec((1, tm // MOBA_BLOCK, tn),
                         lambda i, j: (i // tiles_per_seq, i % tiles_per_seq, j)),
        ],
        out_shape=[
            jax.ShapeDtypeStruct((batch, n_pairs, LANES_V7X, seq), BF16),
            jax.ShapeDtypeStruct((batch, n_pairs, seq, 2 * LANES_V7X), BF16),
            jax.ShapeDtypeStruct((batch, N_HEADS, v_rows, seq), BF16),
            jax.ShapeDtypeStruct((batch, seq // MOBA_BLOCK, d), F32),
        ],
        compiler_params=_params(2),
        name="moba_qkv_proj",
    )(x, w_qkv, w_qkv, w_qkv)


def _moba_attn_kernel(q_ref, k_ref, v_ref, kbar_ref, tab_ref, o_ref, s_ref, *, head_dim):
    blk_len = MOBA_BLOCK
    tq = pair_len = 2 * blk_len
    width = 2 * tq
    n_tiles = q_ref.shape[3] // tq

    def slots(tile):
        return 0 if tile % 2 == 0 else n_tiles - 1

    def score_unit(tile, pair, q_aug, cmax):
        s = jnp.dot(k_ref[0, 0, pair * pair_len:(pair + 1) * pair_len, :], q_aug,
                    preferred_element_type=F32)
        if pair == tile:
            s = s + tab_ref[0, blk_len:]
        elif pair == tile - 1:
            s = jnp.concatenate([s[:blk_len], s[blk_len:] + tab_ref[0, :blk_len]], axis=0)
        s_ref[slots(tile) + pair] = s
        folded = jnp.max(s.reshape(pair_len // SUBLANES_V7X, SUBLANES_V7X, width), axis=0)
        return folded if cmax is None else jnp.maximum(cmax, folded)

    def value_unit(tile, first, count, col_m, acc):
        keys = slice(first * pair_len, (first + count) * pair_len)
        base = slots(tile) + first
        out = []
        for h in range(2):
            cols = slice(h * tq, (h + 1) * tq)
            s = s_ref[base:base + count, :, cols].reshape(count * pair_len, tq)
            p = jnp.exp2((s - col_m[:, cols]).astype(BF16))
            pv = jnp.dot(v_ref[0, h, :, keys], p, preferred_element_type=F32)
            out.append(pv if acc is None else acc[h] + pv)
        return out

    def finish(tile, acc):
        heads = [a[:head_dim] / a[head_dim:head_dim + 1] for a in acc]
        o_ref[0, tile * tq:(tile + 1) * tq, :] = (
            jnp.concatenate(heads, axis=0).T.astype(o_ref.dtype))

    col_m = None
    for step in range(n_tiles + 1):
        scoring, valuing = step, step - 1
        score_pairs = list(range(scoring + 1)) if scoring < n_tiles else []
        value_groups = ([(first, min(2, valuing + 1 - first)) for first in range(0, valuing + 1, 2)]
                        if valuing >= 0 else [])
        q_aug = _tile_queries(scoring, q_ref, kbar_ref, head_dim) if score_pairs else None
        cmax = acc = None
        while score_pairs or value_groups:
            for pair in score_pairs[:2]:
                cmax = score_unit(scoring, pair, q_aug, cmax)
            score_pairs = score_pairs[2:]
            if value_groups:
                acc = value_unit(valuing, *value_groups.pop(0), col_m, acc)
        if valuing >= 0:
            finish(valuing, acc)
        if cmax is not None:
            col_m = jnp.max(cmax, axis=0, keepdims=True)


def _tile_queries(tile, q_ref, kbar_ref, head_dim):
    blk_len = MOBA_BLOCK
    tq = 2 * blk_len
    q_t = q_ref[0, 0, :, tile * tq:(tile + 1) * tq]
    n_blocks = kbar_ref.shape[1]
    feat_row = lax.broadcasted_iota(jnp.int32, q_t.shape, 0)
    kbar = kbar_ref[0].astype(BF16)
    blk = lax.broadcasted_iota(jnp.int32, (n_blocks, tq), 0)
    query = lax.broadcasted_iota(jnp.int32, (n_blocks, tq), 1)
    cur = 2 * tile + jnp.where(query >= blk_len, 1, 0)

    q_aug = []
    for h in range(2):
        in_head = (feat_row >= head_dim) if h else (feat_row < head_dim)
        qm = jnp.where(in_head, q_t, jnp.zeros_like(q_t))
        gate = jnp.dot(kbar, qm, preferred_element_type=F32)
        gate = jnp.where(blk < cur, gate, -jnp.inf)
        rank = jnp.zeros(gate.shape, F32)
        for m in range(min(n_blocks, 2 * tile + 1)):
            gm = gate[m:m + 1, :]
            tie = jnp.where(blk > m, 1.0, 0.0)
            rank = rank + jnp.where(gm > gate, 1.0, jnp.where(gm == gate, tie, 0.0))
        feat = jnp.where(blk < cur,
                         jnp.where(rank < MOBA_TOPK, 0.0, MASK_NEG),
                         jnp.where(blk == cur, 0.0, MASK_NEG))
        feat = jnp.concatenate(
            [feat, jnp.zeros((LANES_V7X - n_blocks, tq), F32)], axis=0)
        q_aug.append(jnp.concatenate([qm, feat.astype(BF16)], axis=0))
    return jnp.concatenate(q_aug, axis=1)


def _rel_bucket(dist):
    n = jnp.maximum(dist, 0)
    max_exact = REL_BUCKETS // 2
    nf = jnp.maximum(n, 1).astype(F32)
    large = max_exact + (jnp.log(nf / max_exact) / math.log(REL_MAX_DIST / max_exact)
                         * (REL_BUCKETS - max_exact)).astype(jnp.int32)
    large = jnp.minimum(large, REL_BUCKETS - 1)
    return jnp.where(n < max_exact, n, large)


def _bias_tables(rel_bias):
    blk = MOBA_BLOCK
    n_keys, n_q = 3 * blk, 2 * blk
    n_heads = rel_bias.shape[1]
    bias_h = rel_bias.T
    d_min = blk - (n_keys - 1)
    dist = jnp.arange(d_min, blk + n_q)
    by_dist = (bias_h[:, _rel_bucket(dist)] - bias_h[:, REL_BUCKETS - 1:]) * LOG2_E
    by_dist = jnp.where(dist[None] < 0, MASK_NEG, by_dist)
    period = by_dist.shape[1] + 1
    assert period % LANES_V7X == 0 and (period - n_q) % LANES_V7X == 0
    rows = jnp.pad(by_dist, ((0, 0), (0, 1)))[:, None, :]
    shift = (period - n_q) - (blk - d_min)

    def expand(rows_ref, out_ref):
        for h in range(2):
            wide = jnp.broadcast_to(rows_ref[h], (n_keys, period))
            rolled = pltpu.roll(wide, shift, 1, stride=1, stride_axis=0)
            out_ref[0, :, h * n_q:(h + 1) * n_q] = rolled[:, period - n_q:]

    return pl.pallas_call(
        expand,
        grid=(n_heads // 2,),
        in_specs=[pl.BlockSpec((2, 1, period), lambda p: (p, 0, 0))],
        out_specs=pl.BlockSpec((1, n_keys, 2 * n_q), lambda p: (p, 0, 0)),
        out_shape=jax.ShapeDtypeStruct((n_heads // 2, n_keys, 2 * n_q), F32),
        compiler_params=_params(1),
        name="rel_bias_tables",
    )(rows)


def _moba_attention(q_t, kaug, v_t, kbar, tables, *, batch, seq):
    n_pairs = q_t.shape[1]
    d = n_pairs * LANES_V7X
    tq = 2 * MOBA_BLOCK
    n_blocks = seq // MOBA_BLOCK
    v_rows = v_t.shape[2]
    return pl.pallas_call(
        functools.partial(_moba_attn_kernel, head_dim=d // N_HEADS),
        grid=(n_pairs, batch),
        in_specs=[
            pl.BlockSpec((1, 1, LANES_V7X, seq), lambda p, b: (b, p, 0, 0)),
            pl.BlockSpec((1, 1, seq, 2 * LANES_V7X), lambda p, b: (b, p, 0, 0)),
            pl.BlockSpec((1, 2, v_rows, seq), lambda p, b: (b, p, 0, 0)),
            pl.BlockSpec((1, n_blocks, LANES_V7X), lambda p, b: (b, 0, p)),
            pl.BlockSpec((1,) + tables.shape[1:], lambda p, b: (p, 0, 0),
                         pipeline_mode=pl.Buffered(1)),
        ],
        out_specs=pl.BlockSpec((1, seq, LANES_V7X), lambda p, b: (b, 0, p)),
        out_shape=jax.ShapeDtypeStruct((batch, seq, d), BF16),
        scratch_shapes=[
            pltpu.VMEM((2 * (seq // tq) - 1, tq, 2 * tq), F32),
        ],
        compiler_params=_params(2),
        name="moba_attention",
    )(q_t, kaug, v_t, kbar, tables)


def _tiles(seq):
    assert seq % (SUBLANES_V7X * MOBA_BLOCK) == 0, "qkv tile must hold 8 whole key blocks"
    return dict(tm_mix=min(seq, 512), tm_ffn=min(seq, 512), tm_qkv=SUBLANES_V7X * MOBA_BLOCK,
                tn_qkv=2 * LANES_V7X, fc_ffn=2 * LANES_V7X)


def kernel(x, ln_mix_g, ln_mix_b, ln_ffn_g, ln_ffn_b, a_w_in, a_ln_g, a_ln_b, a_w_s, a_b_s,
           a_w_out, b_w_qkv, b_w_o, rel_bias, f_w_up, f_conv_w, f_conv_b, f_w_down):
    batch, seq, d = x.shape
    depth = ln_mix_g.shape[0]
    alpha = (2 * depth) ** 0.25
    cfg = _tiles(seq)
    width = a_w_out.shape[1]
    n_groups = a_w_s.shape[1]

    def rows(p):
        return p[:, None, :]

    a_w_in, a_w_out, b_w_qkv, b_w_o, f_w_up, f_w_down = (
        w.astype(BF16) for w in (a_w_in, a_w_out, b_w_qkv, b_w_o, f_w_up, f_w_down))
    ln_mix_g, ln_mix_b, ln_ffn_g, ln_ffn_b, a_ln_g, a_ln_b, f_conv_b = (
        rows(p) for p in (ln_mix_g, ln_mix_b, ln_ffn_g, ln_ffn_b, a_ln_g, a_ln_b, f_conv_b))
    bs_tile = jnp.repeat(a_b_s.transpose(0, 2, 1), width // n_groups, axis=2)
    tables = _bias_tables(rel_bias)

    h = x.reshape(batch * seq, d)
    for i in range(depth):
        j = i // 2
        if i % 2 == 0:
            mix = _gmlp_mixer(h, a_w_in, a_ln_g, a_ln_b, a_w_s, bs_tile, j, tm=cfg["tm_mix"])
            w_proj = a_w_out
        else:
            q_t, kaug, v_t, kbar = _qkv_proj(h, b_w_qkv, j, batch=batch, seq=seq,
                                             tm=cfg["tm_qkv"], tn=cfg["tn_qkv"])
            mix = _moba_attention(q_t, kaug, v_t, kbar, tables, batch=batch, seq=seq)
            mix = mix.reshape(batch * seq, d)
            w_proj = b_w_o
        h = _channel_stage(mix, h, w_proj, j, ln_mix_g, ln_mix_b, f_w_up, f_conv_w, f_conv_b,
                           f_w_down, ln_ffn_g, ln_ffn_b, i, alpha=alpha, tm=cfg["tm_ffn"],
                           fc=cfg["fc_ffn"], seq=seq)
    return h.reshape(batch, seq, d)
```

```python
import functools
import math

import jax
import jax.numpy as jnp
from jax import lax
from jax.experimental import pallas as pl
from jax.experimental.pallas import tpu as pltpu

F32 = jnp.float32
BF16 = jnp.bfloat16

GMLP_CHUNK = 128
N_HEADS = 16
MOBA_BLOCK = 256
MOBA_TOPK = 3
REL_BUCKETS = 32
REL_MAX_DIST = 128
LN_EPS = 1e-5

LANES_V7X = 128
SUBLANES_V7X = 8
BF16_SUBLANES_V7X = 16
VMEM_LIMIT_BYTES_V7X = 56 * 1024 * 1024

MASK_NEG = -1e30
LOG2_E = math.log2(math.e)


def _layer_norm(x, g, b):
    mu = jnp.mean(x, axis=-1, keepdims=True)
    xc = x - mu
    var = jnp.mean(xc * xc, axis=-1, keepdims=True)
    return xc * lax.rsqrt(var + LN_EPS) * g + b


def _layer_block(stacked, layer):
    tail = stacked.shape[1:]
    return pl.BlockSpec((None,) + tail, lambda *_: (layer,) + (0,) * len(tail),
                        pipeline_mode=pl.Buffered(1))


def _resident(array):
    return pl.BlockSpec(array.shape, lambda *_: (0,) * array.ndim, pipeline_mode=pl.Buffered(1))


def _params(n_axes):
    return pltpu.CompilerParams(
        dimension_semantics=("arbitrary",) * n_axes,
        vmem_limit_bytes=VMEM_LIMIT_BYTES_V7X,
    )


class _CastRide:
    def __init__(self, sources, n_chunks, chunk_of):
        rows = n_chunks * BF16_SUBLANES_V7X
        self.args, self.in_specs, self.out_specs, self.out_shapes, self.shapes = [], [], [], [], []
        for stacked, layer in sources:
            n_layers, r, c = stacked.shape
            cols = r * c // rows
            assert rows * cols == r * c and cols % LANES_V7X == 0, (stacked.shape, rows)
            self.args.append(stacked.reshape(n_layers, rows, cols))
            self.in_specs.append(pl.BlockSpec(
                (None, BF16_SUBLANES_V7X, cols),
                lambda *g, layer=layer: (layer, chunk_of(*g), 0)))
            self.out_specs.append(pl.BlockSpec(
                (BF16_SUBLANES_V7X, cols), lambda *g: (chunk_of(*g), 0)))
            self.out_shapes.append(jax.ShapeDtypeStruct((rows, cols), BF16))
            self.shapes.append((r, c))
        self.n = len(self.args)

    def restore(self, outs):
        return [o.reshape(shape) for o, shape in zip(outs, self.shapes)]


def _cast_chunks(src_refs, dst_refs):
    for src, dst in zip(src_refs, dst_refs, strict=True):
        dst[...] = src[...].astype(dst.dtype)


def _gmlp_kernel(x_ref, w_in_ref, lng_ref, lnb_ref, ws_ref, bs_ref, *rest, n_groups, n_cast):
    cast_src, (o_ref, *cast_dst) = rest[:n_cast], rest[n_cast:]
    _cast_chunks(cast_src, cast_dst)
    x = x_ref[...]
    tm = x.shape[0]
    width = o_ref.shape[1]
    gdim = width // n_groups
    z = jnp.dot(x.astype(BF16), w_in_ref[...], preferred_element_type=F32)
    z = jax.nn.gelu(z)
    u = z[:, :width]
    v = _layer_norm(z[:, width:], lng_ref[...], lnb_ref[...]).astype(BF16)
    row = lax.broadcasted_iota(jnp.int32, (GMLP_CHUNK, GMLP_CHUNK), 0)
    col = lax.broadcasted_iota(jnp.int32, (GMLP_CHUNK, GMLP_CHUNK), 1)
    causal = col <= row
    w_masked = [jnp.where(causal, ws_ref[g], 0.0).astype(BF16) for g in range(n_groups)]
    chunks = []
    for c in range(tm // GMLP_CHUNK):
        rows = slice(c * GMLP_CHUNK, (c + 1) * GMLP_CHUNK)
        groups = [
            jnp.dot(w_masked[g], v[rows, g * gdim:(g + 1) * gdim], preferred_element_type=F32)
            for g in range(n_groups)
        ]
        chunks.append(jnp.concatenate(groups, axis=1) + bs_ref[...])
    sv = jnp.concatenate(chunks, axis=0)
    o_ref[...] = (u * sv).astype(o_ref.dtype)


def _gmlp_mixer(x, w_in, ln_g, ln_b, w_s, bs_tile, layer, cast_sources, *, tm):
    t, d = x.shape
    width = w_in.shape[1] // 2
    n_steps = t // tm
    ride = _CastRide(cast_sources, n_steps, lambda i: i)
    mix, *casts = pl.pallas_call(
        functools.partial(_gmlp_kernel, n_groups=w_s.shape[1], n_cast=ride.n),
        grid=(n_steps,),
        in_specs=[pl.BlockSpec((tm, d), lambda i: (i, 0)), _resident(w_in)]
        + [_layer_block(a, layer) for a in (ln_g, ln_b, w_s, bs_tile)] + ride.in_specs,
        out_specs=[pl.BlockSpec((tm, width), lambda i: (i, 0))] + ride.out_specs,
        out_shape=[jax.ShapeDtypeStruct((t, width), BF16)] + ride.out_shapes,
        compiler_params=_params(1),
        name="gmlp_mixer",
    )(x, w_in, ln_g, ln_b, w_s, bs_tile, *ride.args)
    return mix, ride.restore(casts)


def _causal_conv(h, prev, cw, cb):
    head = jnp.concatenate([prev, h[:SUBLANES_V7X]], axis=0)

    def shifted(rows):
        top = pltpu.roll(head, rows, 0)[SUBLANES_V7X:]
        return jnp.concatenate([top, pltpu.roll(h, rows, 0)[SUBLANES_V7X:]], axis=0).astype(BF16)

    cw = cw.astype(BF16)
    return (cb.astype(BF16) + cw[0:1] * shifted(2) + cw[1:2] * shifted(1)
            + cw[2:3] * h.astype(BF16))


def _channel_kernel(mix0_ref, x0_ref, mix_ref, x_ref, w_proj_ref, g_mix_ref, b_mix_ref,
                    w_up_ref, cw_ref, cb_ref, w_down_ref, g_ffn_ref, b_ffn_ref, *rest,
                    alpha, fc, tiles_per_seq, n_tiles, n_cast):
    cast_src, (o_ref, *rest) = rest[:n_cast], rest[n_cast:]
    cast_dst, (xmid_ref, xnext_ref, ysum_ref, act_ref, carry_ref) = rest[:n_cast], rest[n_cast:]
    _cast_chunks(cast_src, cast_dst)
    step = pl.program_id(0)
    tm = x_ref.shape[0]
    ffn = w_down_ref.shape[0]

    def mixer_sublayer(mix, x):
        y = jnp.dot(mix, w_proj_ref[...], preferred_element_type=F32)
        return _layer_norm(alpha * x + y, g_mix_ref[...], b_mix_ref[...])

    def final_norm():
        return _layer_norm(ysum_ref[...], g_ffn_ref[...], b_ffn_ref[...])

    @pl.when(step == 0)
    def _():
        xmid_ref[...] = mixer_sublayer(mix0_ref[...], x0_ref[...])
        ysum_ref[...] = jnp.zeros_like(ysum_ref)

    @pl.when(step < n_tiles)
    def _():
        @pl.when(step % tiles_per_seq == 0)
        def _():
            carry_ref[...] = jnp.zeros_like(carry_ref)

        x = xmid_ref[...]
        xb = x.astype(BF16)

        def up_proj(j):
            return tuple(
                jnp.dot(xb, w_up_ref[:, start:start + fc], preferred_element_type=F32)
                for start in (j * fc, ffn + j * fc))

        def conv_cols(h, start):
            cols = slice(start, start + fc)
            prev = carry_ref[:, cols]
            carry_ref[:, cols] = h[tm - SUBLANES_V7X:]
            return _causal_conv(h, prev, cw_ref[:, cols], cb_ref[:, cols])

        n_chunks = ffn // fc
        h_pair = up_proj(0)
        for j in range(n_chunks):
            h_gate, h_val = h_pair
            if j + 1 < n_chunks:
                h_pair = up_proj(j + 1)
            else:
                xnext_ref[...] = mixer_sublayer(mix_ref[...], x_ref[...])
            gate = conv_cols(h_gate, j * fc)
            val = conv_cols(h_val, ffn + j * fc)
            act_ref[:, j * fc:(j + 1) * fc] = jax.nn.gelu(gate) * val

        o_ref[...] = final_norm()
        y = jnp.dot(act_ref[...], w_down_ref[...], preferred_element_type=F32)
        ysum_ref[...] = alpha * x + y
        xmid_ref[...] = xnext_ref[...]

    @pl.when(step == n_tiles)
    def _():
        o_ref[...] = final_norm()


def _channel_stage(mix, x, w_proj, ln_mix_g, ln_mix_b, w_up, conv_w, conv_b, w_down,
                   ln_ffn_g, ln_ffn_b, layer, cast_sources, *, alpha, tm, fc, seq):
    t, d = x.shape
    ffn = w_down.shape[0]
    n_tiles = t // tm
    width = mix.shape[1]
    ride = _CastRide(cast_sources, n_tiles, lambda i: jnp.minimum(i, n_tiles - 1))

    def first_tile(cols):
        return pl.BlockSpec((tm, cols), lambda i: (0, 0), pipeline_mode=pl.Buffered(1))

    def next_tile(cols):
        return pl.BlockSpec((tm, cols), lambda i: (jnp.minimum(i + 1, n_tiles - 1), 0))

    def layer_rows(p):
        return _layer_block(p, layer)

    out, *casts = pl.pallas_call(
        functools.partial(_channel_kernel, alpha=alpha, fc=fc, tiles_per_seq=seq // tm,
                          n_tiles=n_tiles, n_cast=ride.n),
        grid=(n_tiles + 1,),
        in_specs=[first_tile(width), first_tile(d), next_tile(width), next_tile(d),
                  _resident(w_proj), layer_rows(ln_mix_g), layer_rows(ln_mix_b),
                  _resident(w_up), layer_rows(conv_w), layer_rows(conv_b), _resident(w_down),
                  layer_rows(ln_ffn_g), layer_rows(ln_ffn_b)] + ride.in_specs,
        out_specs=[pl.BlockSpec((tm, d), lambda i: (jnp.maximum(i - 1, 0), 0))] + ride.out_specs,
        out_shape=[jax.ShapeDtypeStruct((t, d), F32)] + ride.out_shapes,
        scratch_shapes=[
            pltpu.VMEM((tm, d), F32),
            pltpu.VMEM((tm, d), F32),
            pltpu.VMEM((tm, d), F32),
            pltpu.VMEM((tm, ffn), BF16),
            pltpu.VMEM((SUBLANES_V7X, 2 * ffn), F32),
        ],
        compiler_params=_params(1),
        name="channel_stage",
    )(mix, x, mix, x, w_proj, ln_mix_g, ln_mix_b, w_up, conv_w, conv_b, w_down,
      ln_ffn_g, ln_ffn_b, *ride.args)
    return out, ride.restore(casts)


def _qkv_kernel(x_ref, wq_ref, wk_ref, wv_ref, q_ref, k_ref, v_ref, kbar_ref,
                *, head_dim, tiles_per_seq):
    q_scale = head_dim ** -0.5 * LOG2_E
    tm = x_ref.shape[0]
    tn = wq_ref.shape[1]
    n_pairs = tn // LANES_V7X
    n_heads = tn // head_dim
    tile_pos = (pl.program_id(0) % tiles_per_seq) * tm
    sub = 2 * MOBA_BLOCK
    lane = lax.broadcasted_iota(jnp.int32, (MOBA_BLOCK, LANES_V7X), 1)

    def project(r):
        xb = x_ref[r * sub:(r + 1) * sub, :].astype(BF16)
        return tuple(jnp.dot(xb, w_ref[...], preferred_element_type=F32)
                     for w_ref in (wq_ref, wk_ref, wv_ref))

    kbar = []
    n_sub = tm // sub
    projected = project(0)
    for r in range(n_sub):
        rows = slice(r * sub, (r + 1) * sub)
        q, k, v = projected
        if r + 1 < n_sub:
            projected = project(r + 1)
        q = q * q_scale
        first_block = (tile_pos + r * sub) // MOBA_BLOCK
        block_onehot = jnp.concatenate(
            [jnp.where(lane == first_block + b, 1.0, 0.0).astype(BF16)
             for b in range(sub // MOBA_BLOCK)], axis=0)
        q_t = q.T.astype(BF16)
        v_t = v.T.astype(BF16)
        for p in range(n_pairs):
            cols = slice(p * LANES_V7X, (p + 1) * LANES_V7X)
            q_ref[0, p, :, rows] = q_t[cols]
            k_ref[0, p, rows, :LANES_V7X] = k[:, cols].astype(BF16)
            k_ref[0, p, rows, LANES_V7X:] = block_onehot
        for h in range(n_heads):
            v_ref[0, h, :head_dim, rows] = v_t[h * head_dim:(h + 1) * head_dim]
            v_ref[0, h, head_dim:, rows] = jnp.ones((v_ref.shape[2] - head_dim, sub), BF16)
        kbar.append(jnp.mean(k.reshape(sub // MOBA_BLOCK, MOBA_BLOCK, tn), axis=1))
    kbar_ref[0] = jnp.concatenate(kbar, axis=0)


def _qkv_proj(x, w_qkv, *, batch, seq, tm, tn):
    t, d = x.shape
    n_pairs = d // LANES_V7X
    pairs_per_step = tn // LANES_V7X
    tiles_per_seq = seq // tm
    n_col = d // tn
    head_dim = d // N_HEADS
    heads_per_step = tn // head_dim
    v_rows = head_dim + BF16_SUBLANES_V7X

    def row_major_map(i, j):
        return (i // tiles_per_seq, j, i % tiles_per_seq, 0)

    def feature_major_map(i, j):
        return (i // tiles_per_seq, j, 0, i % tiles_per_seq)

    def w_cols(which):
        return pl.BlockSpec((d, tn), lambda i, j: (0, which * n_col + j))

    return pl.pallas_call(
        functools.partial(_qkv_kernel, head_dim=head_dim, tiles_per_seq=tiles_per_seq),
        grid=(t // tm, n_col),
        in_specs=[pl.BlockSpec((tm, d), lambda i, j: (i, 0)), w_cols(0), w_cols(1), w_cols(2)],
        out_specs=[
            pl.BlockSpec((1, pairs_per_step, LANES_V7X, tm), feature_major_map),
            pl.BlockSpec((1, pairs_per_step, tm, 2 * LANES_V7X), row_major_map),
            pl.BlockSpec((1, heads_per_step, v_rows, tm), feature_major_map),
            pl.BlockSpec((1, tm // MOBA_BLOCK, tn),
                         lambda i, j: (i // tiles_per_seq, i % tiles_per_seq, j)),
        ],
        out_shape=[
            jax.ShapeDtypeStruct((batch, n_pairs, LANES_V7X, seq), BF16),
            jax.ShapeDtypeStruct((batch, n_pairs, seq, 2 * LANES_V7X), BF16),
            jax.ShapeDtypeStruct((batch, N_HEADS, v_rows, seq), BF16),
            jax.ShapeDtypeStruct((batch, seq // MOBA_BLOCK, d), F32),
        ],
        compiler_params=_params(2),
        name="moba_qkv_proj",
    )(x, w_qkv, w_qkv, w_qkv)


def _moba_attn_kernel(q_ref, k_ref, v_ref, kbar_ref, tab_ref, *rest, head_dim, n_cast):
    cast_src, (o_ref, *cast_dst, s_ref) = rest[:n_cast], rest[n_cast:]
    _cast_chunks(cast_src, cast_dst)
    blk_len = MOBA_BLOCK
    tq = pair_len = 2 * blk_len
    width = 2 * tq
    n_tiles = q_ref.shape[3] // tq

    def slots(tile):
        return 0 if tile % 2 == 0 else n_tiles - 1

    def score_unit(tile, pair, q_aug, cmax):
        s = jnp.dot(k_ref[0, 0, pair * pair_len:(pair + 1) * pair_len, :], q_aug,
                    preferred_element_type=F32)
        if pair == tile:
            s = s + tab_ref[0, blk_len:]
        elif pair == tile - 1:
            s = jnp.concatenate([s[:blk_len], s[blk_len:] + tab_ref[0, :blk_len]], axis=0)
        s_ref[slots(tile) + pair] = s
        folded = jnp.max(s.reshape(pair_len // SUBLANES_V7X, SUBLANES_V7X, width), axis=0)
        return folded if cmax is None else jnp.maximum(cmax, folded)

    def value_unit(tile, first, count, col_m, acc):
        keys = slice(first * pair_len, (first + count) * pair_len)
        base = slots(tile) + first
        out = []
        for h in range(2):
            cols = slice(h * tq, (h + 1) * tq)
            s = s_ref[base:base + count, :, cols].reshape(count * pair_len, tq)
            p = jnp.exp2((s - col_m[:, cols]).astype(BF16))
            pv = jnp.dot(v_ref[0, h, :, keys], p, preferred_element_type=F32)
            out.append(pv if acc is None else acc[h] + pv)
        return out

    def finish(tile, acc):
        heads = [a[:head_dim] / a[head_dim:head_dim + 1] for a in acc]
        o_ref[0, tile * tq:(tile + 1) * tq, :] = (
            jnp.concatenate(heads, axis=0).T.astype(o_ref.dtype))

    col_m = None
    q_aug_next = _tile_queries(0, q_ref, kbar_ref, head_dim)
    for step in range(n_tiles + 1):
        scoring, valuing = step, step - 1
        score_pairs = list(range(scoring + 1)) if scoring < n_tiles else []
        value_groups = ([(first, min(2, valuing + 1 - first)) for first in range(0, valuing + 1, 2)]
                        if valuing >= 0 else [])
        q_aug, q_aug_next = q_aug_next, None
        cmax = acc = None
        while score_pairs or value_groups:
            for pair in score_pairs[:2]:
                cmax = score_unit(scoring, pair, q_aug, cmax)
            score_pairs = score_pairs[2:]
            if q_aug_next is None and scoring + 1 < n_tiles:
                q_aug_next = _tile_queries(scoring + 1, q_ref, kbar_ref, head_dim)
            if value_groups:
                acc = value_unit(valuing, *value_groups.pop(0), col_m, acc)
        if valuing >= 0:
            finish(valuing, acc)
        if cmax is not None:
            col_m = jnp.max(cmax, axis=0, keepdims=True)


def _tile_queries(tile, q_ref, kbar_ref, head_dim):
    blk_len = MOBA_BLOCK
    tq = 2 * blk_len
    q_t = q_ref[0, 0, :, tile * tq:(tile + 1) * tq]
    n_blocks = kbar_ref.shape[1]
    feat_row = lax.broadcasted_iota(jnp.int32, q_t.shape, 0)
    kbar = kbar_ref[0].astype(BF16)
    blk = lax.broadcasted_iota(jnp.int32, (n_blocks, tq), 0)
    query = lax.broadcasted_iota(jnp.int32, (n_blocks, tq), 1)
    cur = 2 * tile + jnp.where(query >= blk_len, 1, 0)

    q_aug = []
    for h in range(2):
        in_head = (feat_row >= head_dim) if h else (feat_row < head_dim)
        qm = jnp.where(in_head, q_t, jnp.zeros_like(q_t))
        gate = jnp.dot(kbar, qm, preferred_element_type=F32)
        gate = jnp.where(blk < cur, gate, -jnp.inf)
        rank = jnp.zeros(gate.shape, F32)
        for m in range(min(n_blocks, 2 * tile + 1)):
            gm = gate[m:m + 1, :]
            tie = jnp.where(blk > m, 1.0, 0.0)
            rank = rank + jnp.where(gm > gate, 1.0, jnp.where(gm == gate, tie, 0.0))
        feat = jnp.where(blk < cur,
                         jnp.where(rank < MOBA_TOPK, 0.0, MASK_NEG),
                         jnp.where(blk == cur, 0.0, MASK_NEG))
        feat = jnp.concatenate(
            [feat, jnp.zeros((LANES_V7X - n_blocks, tq), F32)], axis=0)
        q_aug.append(jnp.concatenate([qm, feat.astype(BF16)], axis=0))
    return jnp.concatenate(q_aug, axis=1)


def _rel_bucket(dist):
    n = jnp.maximum(dist, 0)
    max_exact = REL_BUCKETS // 2
    nf = jnp.maximum(n, 1).astype(F32)
    large = max_exact + (jnp.log(nf / max_exact) / math.log(REL_MAX_DIST / max_exact)
                         * (REL_BUCKETS - max_exact)).astype(jnp.int32)
    large = jnp.minimum(large, REL_BUCKETS - 1)
    return jnp.where(n < max_exact, n, large)


def _bias_tables(rel_bias):
    blk = MOBA_BLOCK
    n_keys, n_q = 3 * blk, 2 * blk
    n_heads = rel_bias.shape[1]
    bias_h = rel_bias.T
    d_min = blk - (n_keys - 1)
    dist = jnp.arange(d_min, blk + n_q)
    by_dist = (bias_h[:, _rel_bucket(dist)] - bias_h[:, REL_BUCKETS - 1:]) * LOG2_E
    by_dist = jnp.where(dist[None] < 0, MASK_NEG, by_dist)
    period = by_dist.shape[1] + 1
    assert period % LANES_V7X == 0 and (period - n_q) % LANES_V7X == 0
    rows = jnp.pad(by_dist, ((0, 0), (0, 1)))[:, None, :]
    shift = (period - n_q) - (blk - d_min)

    def expand(rows_ref, out_ref):
        for h in range(2):
            wide = jnp.broadcast_to(rows_ref[h], (n_keys, period))
            rolled = pltpu.roll(wide, shift, 1, stride=1, stride_axis=0)
            out_ref[0, :, h * n_q:(h + 1) * n_q] = rolled[:, period - n_q:]

    return pl.pallas_call(
        expand,
        grid=(n_heads // 2,),
        in_specs=[pl.BlockSpec((2, 1, period), lambda p: (p, 0, 0))],
        out_specs=pl.BlockSpec((1, n_keys, 2 * n_q), lambda p: (p, 0, 0)),
        out_shape=jax.ShapeDtypeStruct((n_heads // 2, n_keys, 2 * n_q), F32),
        compiler_params=_params(1),
        name="rel_bias_tables",
    )(rows)


def _moba_attention(q_t, kaug, v_t, kbar, tables, cast_sources, *, batch, seq):
    n_pairs = q_t.shape[1]
    d = n_pairs * LANES_V7X
    tq = 2 * MOBA_BLOCK
    n_blocks = seq // MOBA_BLOCK
    v_rows = v_t.shape[2]
    ride = _CastRide(cast_sources, n_pairs * batch, lambda p, b: p * batch + b)
    out, *casts = pl.pallas_call(
        functools.partial(_moba_attn_kernel, head_dim=d // N_HEADS, n_cast=ride.n),
        grid=(n_pairs, batch),
        in_specs=[
            pl.BlockSpec((1, 1, LANES_V7X, seq), lambda p, b: (b, p, 0, 0)),
            pl.BlockSpec((1, 1, seq, 2 * LANES_V7X), lambda p, b: (b, p, 0, 0)),
            pl.BlockSpec((1, 2, v_rows, seq), lambda p, b: (b, p, 0, 0)),
            pl.BlockSpec((1, n_blocks, LANES_V7X), lambda p, b: (b, 0, p)),
            pl.BlockSpec((1,) + tables.shape[1:], lambda p, b: (p, 0, 0),
                         pipeline_mode=pl.Buffered(1)),
        ] + ride.in_specs,
        out_specs=[pl.BlockSpec((1, seq, LANES_V7X), lambda p, b: (b, 0, p))] + ride.out_specs,
        out_shape=[jax.ShapeDtypeStruct((batch, seq, d), BF16)] + ride.out_shapes,
        scratch_shapes=[
            pltpu.VMEM((2 * (seq // tq) - 1, tq, 2 * tq), F32),
        ],
        compiler_params=_params(2),
        name="moba_attention",
    )(q_t, kaug, v_t, kbar, tables, *ride.args)
    return out, ride.restore(casts)


def _tiles(seq):
    assert seq % (SUBLANES_V7X * MOBA_BLOCK) == 0, "qkv tile must hold 8 whole key blocks"
    return dict(tm_mix=min(seq, 512), tm_ffn=min(seq, 512), tm_qkv=SUBLANES_V7X * MOBA_BLOCK,
                tn_qkv=2 * LANES_V7X, fc_ffn=2 * LANES_V7X)


def kernel(x, ln_mix_g, ln_mix_b, ln_ffn_g, ln_ffn_b, a_w_in, a_ln_g, a_ln_b, a_w_s, a_b_s,
           a_w_out, b_w_qkv, b_w_o, rel_bias, f_w_up, f_conv_w, f_conv_b, f_w_down):
    batch, seq, d = x.shape
    depth = ln_mix_g.shape[0]
    alpha = (2 * depth) ** 0.25
    cfg = _tiles(seq)
    width = a_w_out.shape[1]
    n_groups = a_w_s.shape[1]

    def rows(p):
        return p[:, None, :]

    ln_mix_g, ln_mix_b, ln_ffn_g, ln_ffn_b, a_ln_g, a_ln_b, f_conv_b = (
        rows(p) for p in (ln_mix_g, ln_mix_b, ln_ffn_g, ln_ffn_b, a_ln_g, a_ln_b, f_conv_b))
    bs_tile = jnp.repeat(a_b_s.transpose(0, 2, 1), width // n_groups, axis=2)
    tables = _bias_tables(rel_bias)

    mixer_w = a_w_in[0].astype(BF16)
    h = x.reshape(batch * seq, d)
    for i in range(depth):
        j = i // 2
        channel_sources = [(a_w_out if i % 2 == 0 else b_w_o, j), (f_w_up, i), (f_w_down, i)]
        if i % 2 == 0:
            mix, channel_w = _gmlp_mixer(h, mixer_w, a_ln_g, a_ln_b, a_w_s, bs_tile, j,
                                         channel_sources, tm=cfg["tm_mix"])
        else:
            q_t, kaug, v_t, kbar = _qkv_proj(h, mixer_w, batch=batch, seq=seq,
                                             tm=cfg["tm_qkv"], tn=cfg["tn_qkv"])
            mix, channel_w = _moba_attention(q_t, kaug, v_t, kbar, tables, channel_sources,
                                             batch=batch, seq=seq)
            mix = mix.reshape(batch * seq, d)
        w_proj, w_up, w_down = channel_w
        next_mixer = [] if i + 1 == depth else [(b_w_qkv, j) if i % 2 == 0 else (a_w_in, j + 1)]
        h, next_w = _channel_stage(mix, h, w_proj, ln_mix_g, ln_mix_b, w_up, f_conv_w, f_conv_b,
                                   w_down, ln_ffn_g, ln_ffn_b, i, next_mixer, alpha=alpha,
                                   tm=cfg["tm_ffn"], fc=cfg["fc_ffn"], seq=seq)
        mixer_w = next_w[0] if next_w else None
    return h.reshape(batch, seq, d)
```

```python
import functools
import math

import jax
import jax.numpy as jnp
from jax import lax
from jax.experimental import pallas as pl
from jax.experimental.pallas import tpu as pltpu

F32 = jnp.float32
BF16 = jnp.bfloat16

GMLP_CHUNK = 128
N_HEADS = 16
MOBA_BLOCK = 256
MOBA_TOPK = 3
REL_BUCKETS = 32
REL_MAX_DIST = 128
LN_EPS = 1e-5

LANES_V7X = 128
SUBLANES_V7X = 8
BF16_SUBLANES_V7X = 16
VMEM_LIMIT_BYTES_V7X = 56 * 1024 * 1024

MASK_NEG = -1e30
LOG2_E = math.log2(math.e)


def _layer_norm(x, g, b):
    mu = jnp.mean(x, axis=-1, keepdims=True)
    xc = x - mu
    var = jnp.mean(xc * xc, axis=-1, keepdims=True)
    return xc * lax.rsqrt(var + LN_EPS) * g + b


def _layer_block(stacked, layer):
    tail = stacked.shape[1:]
    return pl.BlockSpec((None,) + tail, lambda *_: (layer,) + (0,) * len(tail),
                        pipeline_mode=pl.Buffered(1))


def _resident(array):
    return pl.BlockSpec(array.shape, lambda *_: (0,) * array.ndim, pipeline_mode=pl.Buffered(1))


def _params(n_axes):
    return pltpu.CompilerParams(
        dimension_semantics=("arbitrary",) * n_axes,
        vmem_limit_bytes=VMEM_LIMIT_BYTES_V7X,
    )


class _CastRide:
    def __init__(self, sources, n_chunks, chunk_of):
        self.args, self.in_specs, self.out_specs, self.out_shapes = [], [], [], []
        for stacked, layer in sources:
            _, r, c = stacked.shape
            share = next(s for s in range(1, n_chunks + 1)
                         if n_chunks % s == 0 and r % (n_chunks // s * BF16_SUBLANES_V7X) == 0)
            rows = r // (n_chunks // share)
            self.args.append(stacked)
            self.in_specs.append(pl.BlockSpec(
                (None, rows, c),
                lambda *g, layer=layer, share=share: (layer, chunk_of(*g) // share, 0)))
            self.out_specs.append(pl.BlockSpec(
                (rows, c), lambda *g, share=share: (chunk_of(*g) // share, 0)))
            self.out_shapes.append(jax.ShapeDtypeStruct((r, c), BF16))
        self.n = len(self.args)


def _cast_chunks(src_refs, dst_refs):
    for src, dst in zip(src_refs, dst_refs, strict=True):
        dst[...] = src[...].astype(dst.dtype)


def _gmlp_kernel(x_ref, w_in_ref, lng_ref, lnb_ref, ws_ref, bs_ref, *rest, n_groups, n_cast):
    cast_src, (o_ref, *cast_dst) = rest[:n_cast], rest[n_cast:]
    _cast_chunks(cast_src, cast_dst)
    x = x_ref[...]
    tm = x.shape[0]
    width = o_ref.shape[1]
    gdim = width // n_groups
    z = jnp.dot(x.astype(BF16), w_in_ref[...], preferred_element_type=F32)
    z = jax.nn.gelu(z)
    u = z[:, :width]
    v = _layer_norm(z[:, width:], lng_ref[...], lnb_ref[...]).astype(BF16)
    row = lax.broadcasted_iota(jnp.int32, (GMLP_CHUNK, GMLP_CHUNK), 0)
    col = lax.broadcasted_iota(jnp.int32, (GMLP_CHUNK, GMLP_CHUNK), 1)
    causal = col <= row
    w_masked = [jnp.where(causal, ws_ref[g], 0.0).astype(BF16) for g in range(n_groups)]
    chunks = []
    for c in range(tm // GMLP_CHUNK):
        rows = slice(c * GMLP_CHUNK, (c + 1) * GMLP_CHUNK)
        groups = [
            jnp.dot(w_masked[g], v[rows, g * gdim:(g + 1) * gdim], preferred_element_type=F32)
            for g in range(n_groups)
        ]
        chunks.append(jnp.concatenate(groups, axis=1) + bs_ref[...])
    sv = jnp.concatenate(chunks, axis=0)
    o_ref[...] = (u * sv).astype(o_ref.dtype)


def _gmlp_mixer(x, w_in, ln_g, ln_b, w_s, bs_tile, layer, cast_sources, *, tm):
    t, d = x.shape
    width = w_in.shape[1] // 2
    n_steps = t // tm
    ride = _CastRide(cast_sources, n_steps, lambda i: i)
    mix, *casts = pl.pallas_call(
        functools.partial(_gmlp_kernel, n_groups=w_s.shape[1], n_cast=ride.n),
        grid=(n_steps,),
        in_specs=[pl.BlockSpec((tm, d), lambda i: (i, 0)), _resident(w_in)]
        + [_layer_block(a, layer) for a in (ln_g, ln_b, w_s, bs_tile)] + ride.in_specs,
        out_specs=[pl.BlockSpec((tm, width), lambda i: (i, 0))] + ride.out_specs,
        out_shape=[jax.ShapeDtypeStruct((t, width), BF16)] + ride.out_shapes,
        compiler_params=_params(1),
        name="gmlp_mixer",
    )(x, w_in, ln_g, ln_b, w_s, bs_tile, *ride.args)
    return mix, casts


def _causal_conv(h, prev, cw, cb):
    head = jnp.concatenate([prev, h[:SUBLANES_V7X]], axis=0)

    def shifted(rows):
        top = pltpu.roll(head, rows, 0)[SUBLANES_V7X:]
        return jnp.concatenate([top, pltpu.roll(h, rows, 0)[SUBLANES_V7X:]], axis=0).astype(BF16)

    cw = cw.astype(BF16)
    return (cb.astype(BF16) + cw[0:1] * shifted(2) + cw[1:2] * shifted(1)
            + cw[2:3] * h.astype(BF16))


def _channel_kernel(mix0_ref, x0_ref, mix_ref, x_ref, w_proj_ref, g_mix_ref, b_mix_ref,
                    w_up_ref, cw_ref, cb_ref, w_down_ref, g_ffn_ref, b_ffn_ref, *rest,
                    alpha, fc, tiles_per_seq, n_tiles, n_cast):
    cast_src, (o_ref, *rest) = rest[:n_cast], rest[n_cast:]
    cast_dst, (xmid_ref, xnext_ref, ysum_ref, act_ref, carry_ref) = rest[:n_cast], rest[n_cast:]
    _cast_chunks(cast_src, cast_dst)
    step = pl.program_id(0)
    tm = x_ref.shape[0]
    ffn = w_down_ref.shape[0]

    def mixer_sublayer(mix, x):
        y = jnp.dot(mix, w_proj_ref[...], preferred_element_type=F32)
        return _layer_norm(alpha * x + y, g_mix_ref[...], b_mix_ref[...])

    def final_norm():
        return _layer_norm(ysum_ref[...], g_ffn_ref[...], b_ffn_ref[...])

    @pl.when(step == 0)
    def _():
        xmid_ref[...] = mixer_sublayer(mix0_ref[...], x0_ref[...])
        ysum_ref[...] = jnp.zeros_like(ysum_ref)

    @pl.when(step < n_tiles)
    def _():
        @pl.when(step % tiles_per_seq == 0)
        def _():
            carry_ref[...] = jnp.zeros_like(carry_ref)

        x = xmid_ref[...]
        xb = x.astype(BF16)

        def up_proj(j):
            return tuple(
                jnp.dot(xb, w_up_ref[:, start:start + fc], preferred_element_type=F32)
                for start in (j * fc, ffn + j * fc))

        def conv_cols(h, start):
            cols = slice(start, start + fc)
            prev = carry_ref[:, cols]
            carry_ref[:, cols] = h[tm - SUBLANES_V7X:]
            return _causal_conv(h, prev, cw_ref[:, cols], cb_ref[:, cols])

        n_chunks = ffn // fc
        h_pair = up_proj(0)
        for j in range(n_chunks):
            h_gate, h_val = h_pair
            if j + 1 < n_chunks:
                h_pair = up_proj(j + 1)
            else:
                xnext_ref[...] = mixer_sublayer(mix_ref[...], x_ref[...])
            gate = conv_cols(h_gate, j * fc)
            val = conv_cols(h_val, ffn + j * fc)
            act_ref[:, j * fc:(j + 1) * fc] = jax.nn.gelu(gate) * val

        o_ref[...] = final_norm()
        y = jnp.dot(act_ref[...], w_down_ref[...], preferred_element_type=F32)
        ysum_ref[...] = alpha * x + y
        xmid_ref[...] = xnext_ref[...]

    @pl.when(step == n_tiles)
    def _():
        o_ref[...] = final_norm()


def _channel_stage(mix, x, w_proj, ln_mix_g, ln_mix_b, w_up, conv_w, conv_b, w_down,
                   ln_ffn_g, ln_ffn_b, layer, cast_sources, *, alpha, tm, fc, seq):
    t, d = x.shape
    ffn = w_down.shape[0]
    n_tiles = t // tm
    width = mix.shape[1]
    ride = _CastRide(cast_sources, n_tiles, lambda i: jnp.minimum(i, n_tiles - 1))

    def first_tile(cols):
        return pl.BlockSpec((tm, cols), lambda i: (0, 0), pipeline_mode=pl.Buffered(1))

    def next_tile(cols):
        return pl.BlockSpec((tm, cols), lambda i: (jnp.minimum(i + 1, n_tiles - 1), 0))

    def layer_rows(p):
        return _layer_block(p, layer)

    out, *casts = pl.pallas_call(
        functools.partial(_channel_kernel, alpha=alpha, fc=fc, tiles_per_seq=seq // tm,
                          n_tiles=n_tiles, n_cast=ride.n),
        grid=(n_tiles + 1,),
        in_specs=[first_tile(width), first_tile(d), next_tile(width), next_tile(d),
                  _resident(w_proj), layer_rows(ln_mix_g), layer_rows(ln_mix_b),
                  _resident(w_up), layer_rows(conv_w), layer_rows(conv_b), _resident(w_down),
                  layer_rows(ln_ffn_g), layer_rows(ln_ffn_b)] + ride.in_specs,
        out_specs=[pl.BlockSpec((tm, d), lambda i: (jnp.maximum(i - 1, 0), 0))] + ride.out_specs,
        out_shape=[jax.ShapeDtypeStruct((t, d), F32)] + ride.out_shapes,
        scratch_shapes=[
            pltpu.VMEM((tm, d), F32),
            pltpu.VMEM((tm, d), F32),
            pltpu.VMEM((tm, d), F32),
            pltpu.VMEM((tm, ffn), BF16),
            pltpu.VMEM((SUBLANES_V7X, 2 * ffn), F32),
        ],
        compiler_params=_params(1),
        name="channel_stage",
    )(mix, x, mix, x, w_proj, ln_mix_g, ln_mix_b, w_up, conv_w, conv_b, w_down,
      ln_ffn_g, ln_ffn_b, *ride.args)
    return out, casts


def _qkv_kernel(x_ref, wq_ref, wk_ref, wv_ref, *rest, head_dim, tiles_per_seq, n_cast):
    cast_src, (q_ref, k_ref, v_ref, kbar_ref, *cast_dst) = rest[:n_cast], rest[n_cast:]
    _cast_chunks(cast_src, cast_dst)
    q_scale = head_dim ** -0.5 * LOG2_E
    tm = x_ref.shape[0]
    tn = wq_ref.shape[1]
    n_pairs = tn // LANES_V7X
    n_heads = tn // head_dim
    tile_pos = (pl.program_id(0) % tiles_per_seq) * tm
    sub = 2 * MOBA_BLOCK
    lane = lax.broadcasted_iota(jnp.int32, (MOBA_BLOCK, LANES_V7X), 1)

    def project(r):
        xb = x_ref[r * sub:(r + 1) * sub, :].astype(BF16)
        return tuple(jnp.dot(xb, w_ref[...], preferred_element_type=F32)
                     for w_ref in (wq_ref, wk_ref, wv_ref))

    kbar = []
    n_sub = tm // sub
    projected = project(0)
    for r in range(n_sub):
        rows = slice(r * sub, (r + 1) * sub)
        q, k, v = projected
        if r + 1 < n_sub:
            projected = project(r + 1)
        q = q * q_scale
        first_block = (tile_pos + r * sub) // MOBA_BLOCK
        block_onehot = jnp.concatenate(
            [jnp.where(lane == first_block + b, 1.0, 0.0).astype(BF16)
             for b in range(sub // MOBA_BLOCK)], axis=0)
        q_t = q.T.astype(BF16)
        v_t = v.T.astype(BF16)
        for p in range(n_pairs):
            cols = slice(p * LANES_V7X, (p + 1) * LANES_V7X)
            q_ref[0, p, :, rows] = q_t[cols]
            k_ref[0, p, rows, :LANES_V7X] = k[:, cols].astype(BF16)
            k_ref[0, p, rows, LANES_V7X:] = block_onehot
        for h in range(n_heads):
            v_ref[0, h, :head_dim, rows] = v_t[h * head_dim:(h + 1) * head_dim]
            v_ref[0, h, head_dim:, rows] = jnp.ones((v_ref.shape[2] - head_dim, sub), BF16)
        kbar.append(jnp.mean(k.reshape(sub // MOBA_BLOCK, MOBA_BLOCK, tn), axis=1))
    kbar_ref[0] = jnp.concatenate(kbar, axis=0)


def _qkv_proj(x, w_qkv, cast_sources, *, batch, seq, tm, tn):
    t, d = x.shape
    n_pairs = d // LANES_V7X
    pairs_per_step = tn // LANES_V7X
    tiles_per_seq = seq // tm
    n_col = d // tn
    head_dim = d // N_HEADS
    heads_per_step = tn // head_dim
    v_rows = head_dim + BF16_SUBLANES_V7X

    def row_major_map(i, j):
        return (i // tiles_per_seq, j, i % tiles_per_seq, 0)

    def feature_major_map(i, j):
        return (i // tiles_per_seq, j, 0, i % tiles_per_seq)

    def w_cols(which):
        return pl.BlockSpec((d, tn), lambda i, j: (0, which * n_col + j))

    n_rows = t // tm
    ride = _CastRide(cast_sources, n_rows * n_col, lambda i, j: i * n_col + j)
    q_t, kaug, v_t, kbar, *casts = pl.pallas_call(
        functools.partial(_qkv_kernel, head_dim=head_dim, tiles_per_seq=tiles_per_seq,
                          n_cast=ride.n),
        grid=(n_rows, n_col),
        in_specs=[pl.BlockSpec((tm, d), lambda i, j: (i, 0)), w_cols(0), w_cols(1), w_cols(2)]
        + ride.in_specs,
        out_specs=[
            pl.BlockSpec((1, pairs_per_step, LANES_V7X, tm), feature_major_map),
            pl.BlockSpec((1, pairs_per_step, tm, 2 * LANES_V7X), row_major_map),
            pl.BlockSpec((1, heads_per_step, v_rows, tm), feature_major_map),
            pl.BlockSpec((1, tm // MOBA_BLOCK, tn),
                         lambda i, j: (i // tiles_per_seq, i % tiles_per_seq, j)),
        ] + ride.out_specs,
        out_shape=[
            jax.ShapeDtypeStruct((batch, n_pairs, LANES_V7X, seq), BF16),
            jax.ShapeDtypeStruct((batch, n_pairs, seq, 2 * LANES_V7X), BF16),
            jax.ShapeDtypeStruct((batch, N_HEADS, v_rows, seq), BF16),
            jax.ShapeDtypeStruct((batch, seq // MOBA_BLOCK, d), F32),
        ] + ride.out_shapes,
        compiler_params=_params(2),
        name="moba_qkv_proj",
    )(x, w_qkv, w_qkv, w_qkv, *ride.args)
    return (q_t, kaug, v_t, kbar), casts


def _moba_attn_kernel(q_ref, k_ref, v_ref, kbar_ref, tab_ref, o_ref, s_ref, *, head_dim):
    blk_len = MOBA_BLOCK
    tq = pair_len = 2 * blk_len
    width = 2 * tq
    n_tiles = q_ref.shape[3] // tq

    def slots(tile):
        return 0 if tile % 2 == 0 else n_tiles - 1

    def score_unit(tile, pair, q_aug, cmax):
        s = jnp.dot(k_ref[0, 0, pair * pair_len:(pair + 1) * pair_len, :], q_aug,
                    preferred_element_type=F32)
        if pair == tile:
            s = s + tab_ref[0, blk_len:]
        elif pair == tile - 1:
            s = jnp.concatenate([s[:blk_len], s[blk_len:] + tab_ref[0, :blk_len]], axis=0)
        s_ref[slots(tile) + pair] = s
        folded = jnp.max(s.reshape(pair_len // SUBLANES_V7X, SUBLANES_V7X, width), axis=0)
        return folded if cmax is None else jnp.maximum(cmax, folded)

    def value_unit(tile, first, count, col_m, acc):
        keys = slice(first * pair_len, (first + count) * pair_len)
        base = slots(tile) + first
        out = []
        for h in range(2):
            cols = slice(h * tq, (h + 1) * tq)
            s = s_ref[base:base + count, :, cols].reshape(count * pair_len, tq)
            p = jnp.exp2((s - col_m[:, cols]).astype(BF16))
            pv = jnp.dot(v_ref[0, h, :, keys], p, preferred_element_type=F32)
            out.append(pv if acc is None else acc[h] + pv)
        return out

    def finish(tile, acc):
        heads = [a[:head_dim] / a[head_dim:head_dim + 1] for a in acc]
        o_ref[0, tile * tq:(tile + 1) * tq, :] = (
            jnp.concatenate(heads, axis=0).T.astype(o_ref.dtype))

    col_m = None
    q_aug_next = _tile_queries(0, q_ref, kbar_ref, head_dim)
    for step in range(n_tiles + 1):
        scoring, valuing = step, step - 1
        score_pairs = list(range(scoring + 1)) if scoring < n_tiles else []
        value_groups = ([(first, min(2, valuing + 1 - first)) for first in range(0, valuing + 1, 2)]
                        if valuing >= 0 else [])
        q_aug, q_aug_next = q_aug_next, None
        cmax = acc = None
        while score_pairs or value_groups:
            for pair in score_pairs[:2]:
                cmax = score_unit(scoring, pair, q_aug, cmax)
            score_pairs = score_pairs[2:]
            if q_aug_next is None and scoring + 1 < n_tiles:
                q_aug_next = _tile_queries(scoring + 1, q_ref, kbar_ref, head_dim)
            if value_groups:
                acc = value_unit(valuing, *value_groups.pop(0), col_m, acc)
        if valuing >= 0:
            finish(valuing, acc)
        if cmax is not None:
            col_m = jnp.max(cmax, axis=0, keepdims=True)


def _tile_queries(tile, q_ref, kbar_ref, head_dim):
    blk_len = MOBA_BLOCK
    tq = 2 * blk_len
    q_t = q_ref[0, 0, :, tile * tq:(tile + 1) * tq]
    n_blocks = kbar_ref.shape[1]
    feat_row = lax.broadcasted_iota(jnp.int32, q_t.shape, 0)
    kbar = kbar_ref[0].astype(BF16)
    blk = lax.broadcasted_iota(jnp.int32, (n_blocks, tq), 0)
    query = lax.broadcasted_iota(jnp.int32, (n_blocks, tq), 1)
    cur = 2 * tile + jnp.where(query >= blk_len, 1, 0)

    q_aug = []
    for h in range(2):
        in_head = (feat_row >= head_dim) if h else (feat_row < head_dim)
        qm = jnp.where(in_head, q_t, jnp.zeros_like(q_t))
        gate = jnp.dot(kbar, qm, preferred_element_type=F32)
        gate = jnp.where(blk < cur, gate, -jnp.inf)
        rank = jnp.zeros(gate.shape, F32)
        for m in range(min(n_blocks, 2 * tile + 1)):
            gm = gate[m:m + 1, :]
            tie = jnp.where(blk > m, 1.0, 0.0)
            rank = rank + jnp.where(gm > gate, 1.0, jnp.where(gm == gate, tie, 0.0))
        feat = jnp.where(blk < cur,
                         jnp.where(rank < MOBA_TOPK, 0.0, MASK_NEG),
                         jnp.where(blk == cur, 0.0, MASK_NEG))
        feat = jnp.concatenate(
            [feat, jnp.zeros((LANES_V7X - n_blocks, tq), F32)], axis=0)
        q_aug.append(jnp.concatenate([qm, feat.astype(BF16)], axis=0))
    return jnp.concatenate(q_aug, axis=1)


def _rel_bucket(dist):
    n = jnp.maximum(dist, 0)
    max_exact = REL_BUCKETS // 2
    nf = jnp.maximum(n, 1).astype(F32)
    large = max_exact + (jnp.log(nf / max_exact) / math.log(REL_MAX_DIST / max_exact)
                         * (REL_BUCKETS - max_exact)).astype(jnp.int32)
    large = jnp.minimum(large, REL_BUCKETS - 1)
    return jnp.where(n < max_exact, n, large)


def _bias_tables(rel_bias):
    blk = MOBA_BLOCK
    n_keys, n_q = 3 * blk, 2 * blk
    n_heads = rel_bias.shape[1]
    bias_h = rel_bias.T
    d_min = blk - (n_keys - 1)
    dist = jnp.arange(d_min, blk + n_q)
    by_dist = (bias_h[:, _rel_bucket(dist)] - bias_h[:, REL_BUCKETS - 1:]) * LOG2_E
    by_dist = jnp.where(dist[None] < 0, MASK_NEG, by_dist)
    period = by_dist.shape[1] + 1
    assert period % LANES_V7X == 0 and (period - n_q) % LANES_V7X == 0
    rows = jnp.pad(by_dist, ((0, 0), (0, 1)))[:, None, :]
    shift = (period - n_q) - (blk - d_min)

    def expand(rows_ref, out_ref):
        for h in range(2):
            wide = jnp.broadcast_to(rows_ref[h], (n_keys, period))
            rolled = pltpu.roll(wide, shift, 1, stride=1, stride_axis=0)
            out_ref[0, :, h * n_q:(h + 1) * n_q] = rolled[:, period - n_q:]

    return pl.pallas_call(
        expand,
        grid=(n_heads // 2,),
        in_specs=[pl.BlockSpec((2, 1, period), lambda p: (p, 0, 0))],
        out_specs=pl.BlockSpec((1, n_keys, 2 * n_q), lambda p: (p, 0, 0)),
        out_shape=jax.ShapeDtypeStruct((n_heads // 2, n_keys, 2 * n_q), F32),
        compiler_params=_params(1),
        name="rel_bias_tables",
    )(rows)


def _moba_attention(q_t, kaug, v_t, kbar, tables, *, batch, seq):
    n_pairs = q_t.shape[1]
    d = n_pairs * LANES_V7X
    tq = 2 * MOBA_BLOCK
    n_blocks = seq // MOBA_BLOCK
    v_rows = v_t.shape[2]
    return pl.pallas_call(
        functools.partial(_moba_attn_kernel, head_dim=d // N_HEADS),
        grid=(n_pairs, batch),
        in_specs=[
            pl.BlockSpec((1, 1, LANES_V7X, seq), lambda p, b: (b, p, 0, 0)),
            pl.BlockSpec((1, 1, seq, 2 * LANES_V7X), lambda p, b: (b, p, 0, 0)),
            pl.BlockSpec((1, 2, v_rows, seq), lambda p, b: (b, p, 0, 0)),
            pl.BlockSpec((1, n_blocks, LANES_V7X), lambda p, b: (b, 0, p)),
            pl.BlockSpec((1,) + tables.shape[1:], lambda p, b: (p, 0, 0),
                         pipeline_mode=pl.Buffered(1)),
        ],
        out_specs=pl.BlockSpec((1, seq, LANES_V7X), lambda p, b: (b, 0, p)),
        out_shape=jax.ShapeDtypeStruct((batch, seq, d), BF16),
        scratch_shapes=[
            pltpu.VMEM((2 * (seq // tq) - 1, tq, 2 * tq), F32),
        ],
        compiler_params=_params(2),
        name="moba_attention",
    )(q_t, kaug, v_t, kbar, tables)


def _tiles(seq):
    assert seq % (SUBLANES_V7X * MOBA_BLOCK) == 0, "qkv tile must hold 8 whole key blocks"
    return dict(tm_mix=min(seq, 512), tm_ffn=min(seq, 512), tm_qkv=SUBLANES_V7X * MOBA_BLOCK,
                tn_qkv=2 * LANES_V7X, fc_ffn=2 * LANES_V7X)


def kernel(x, ln_mix_g, ln_mix_b, ln_ffn_g, ln_ffn_b, a_w_in, a_ln_g, a_ln_b, a_w_s, a_b_s,
           a_w_out, b_w_qkv, b_w_o, rel_bias, f_w_up, f_conv_w, f_conv_b, f_w_down):
    batch, seq, d = x.shape
    depth = ln_mix_g.shape[0]
    alpha = (2 * depth) ** 0.25
    cfg = _tiles(seq)
    width = a_w_out.shape[1]
    n_groups = a_w_s.shape[1]

    def rows(p):
        return p[:, None, :]

    ln_mix_g, ln_mix_b, ln_ffn_g, ln_ffn_b, a_ln_g, a_ln_b, f_conv_b = (
        rows(p) for p in (ln_mix_g, ln_mix_b, ln_ffn_g, ln_ffn_b, a_ln_g, a_ln_b, f_conv_b))
    bs_tile = jnp.repeat(a_b_s.transpose(0, 2, 1), width // n_groups, axis=2)
    tables = _bias_tables(rel_bias)

    mixer_w = a_w_in[0].astype(BF16)
    h = x.reshape(batch * seq, d)
    for i in range(depth):
        j = i // 2
        channel_sources = [(a_w_out if i % 2 == 0 else b_w_o, j), (f_w_up, i), (f_w_down, i)]
        if i % 2 == 0:
            mix, channel_w = _gmlp_mixer(h, mixer_w, a_ln_g, a_ln_b, a_w_s, bs_tile, j,
                                         channel_sources, tm=cfg["tm_mix"])
        else:
            qkv, channel_w = _qkv_proj(h, mixer_w, channel_sources, batch=batch, seq=seq,
                                       tm=cfg["tm_qkv"], tn=cfg["tn_qkv"])
            mix = _moba_attention(*qkv, tables, batch=batch, seq=seq).reshape(batch * seq, d)
        w_proj, w_up, w_down = channel_w
        next_mixer = [] if i + 1 == depth else [(b_w_qkv, j) if i % 2 == 0 else (a_w_in, j + 1)]
        h, next_w = _channel_stage(mix, h, w_proj, ln_mix_g, ln_mix_b, w_up, f_conv_w, f_conv_b,
                                   w_down, ln_ffn_g, ln_ffn_b, i, next_mixer, alpha=alpha,
                                   tm=cfg["tm_ffn"], fc=cfg["fc_ffn"], seq=seq)
        mixer_w = next_w[0] if next_w else None
    return h.reshape(batch, seq, d)
```

```python
import functools
import math

import jax
import jax.numpy as jnp
from jax import lax
from jax.experimental import pallas as pl
from jax.experimental.pallas import tpu as pltpu

F32 = jnp.float32
BF16 = jnp.bfloat16

GMLP_CHUNK = 128
N_HEADS = 16
MOBA_BLOCK = 256
MOBA_TOPK = 3
REL_BUCKETS = 32
REL_MAX_DIST = 128
LN_EPS = 1e-5

LANES_V7X = 128
SUBLANES_V7X = 8
BF16_SUBLANES_V7X = 16
VMEM_LIMIT_BYTES_V7X = 56 * 1024 * 1024

MASK_NEG = -1e30
LOG2_E = math.log2(math.e)


def _layer_norm(x, g, b):
    mu = jnp.mean(x, axis=-1, keepdims=True)
    xc = x - mu
    var = jnp.mean(xc * xc, axis=-1, keepdims=True)
    return xc * lax.rsqrt(var + LN_EPS) * g + b


def _layer_block(stacked, layer):
    tail = stacked.shape[1:]
    return pl.BlockSpec((None,) + tail, lambda *_: (layer,) + (0,) * len(tail),
                        pipeline_mode=pl.Buffered(1))


def _resident(array):
    return pl.BlockSpec(array.shape, lambda *_: (0,) * array.ndim, pipeline_mode=pl.Buffered(1))


def _params(n_axes):
    return pltpu.CompilerParams(
        dimension_semantics=("arbitrary",) * n_axes,
        vmem_limit_bytes=VMEM_LIMIT_BYTES_V7X,
    )


class _CastRide:
    def __init__(self, sources, n_chunks, chunk_of):
        self.args, self.in_specs, self.out_specs, self.out_shapes = [], [], [], []
        for stacked, layer in sources:
            _, r, c = stacked.shape
            share = next(s for s in range(1, n_chunks + 1)
                         if n_chunks % s == 0 and r % (n_chunks // s * BF16_SUBLANES_V7X) == 0)
            rows = r // (n_chunks // share)
            self.args.append(stacked)
            self.in_specs.append(pl.BlockSpec(
                (None, rows, c),
                lambda *g, layer=layer, share=share: (layer, chunk_of(*g) // share, 0)))
            self.out_specs.append(pl.BlockSpec(
                (rows, c), lambda *g, share=share: (chunk_of(*g) // share, 0)))
            self.out_shapes.append(jax.ShapeDtypeStruct((r, c), BF16))
        self.n = len(self.args)


def _cast_chunks(src_refs, dst_refs):
    for src, dst in zip(src_refs, dst_refs, strict=True):
        dst[...] = src[...].astype(dst.dtype)


def _gmlp_kernel(x_ref, w_in_ref, lng_ref, lnb_ref, ws_ref, bs_ref, *rest, n_groups, n_cast):
    cast_src, (o_ref, *cast_dst) = rest[:n_cast], rest[n_cast:]
    _cast_chunks(cast_src, cast_dst)
    x = x_ref[...]
    tm = x.shape[0]
    width = o_ref.shape[1]
    gdim = width // n_groups
    z = jnp.dot(x.astype(BF16), w_in_ref[...], preferred_element_type=F32)
    z = jax.nn.gelu(z)
    u = z[:, :width]
    v = _layer_norm(z[:, width:], lng_ref[...], lnb_ref[...]).astype(BF16)
    row = lax.broadcasted_iota(jnp.int32, (GMLP_CHUNK, GMLP_CHUNK), 0)
    col = lax.broadcasted_iota(jnp.int32, (GMLP_CHUNK, GMLP_CHUNK), 1)
    causal = col <= row
    w_masked = [jnp.where(causal, ws_ref[g], 0.0).astype(BF16) for g in range(n_groups)]
    chunks = []
    for c in range(tm // GMLP_CHUNK):
        rows = slice(c * GMLP_CHUNK, (c + 1) * GMLP_CHUNK)
        groups = [
            jnp.dot(w_masked[g], v[rows, g * gdim:(g + 1) * gdim], preferred_element_type=F32)
            for g in range(n_groups)
        ]
        chunks.append(jnp.concatenate(groups, axis=1) + bs_ref[...])
    sv = jnp.concatenate(chunks, axis=0)
    o_ref[...] = (u * sv).astype(o_ref.dtype)


def _gmlp_mixer(x, w_in, ln_g, ln_b, w_s, bs_tile, layer, cast_sources, *, tm):
    t, d = x.shape
    width = w_in.shape[1] // 2
    n_steps = t // tm
    ride = _CastRide(cast_sources, n_steps, lambda i: i)
    mix, *casts = pl.pallas_call(
        functools.partial(_gmlp_kernel, n_groups=w_s.shape[1], n_cast=ride.n),
        grid=(n_steps,),
        in_specs=[pl.BlockSpec((tm, d), lambda i: (i, 0)), _resident(w_in)]
        + [_layer_block(a, layer) for a in (ln_g, ln_b, w_s, bs_tile)] + ride.in_specs,
        out_specs=[pl.BlockSpec((tm, width), lambda i: (i, 0))] + ride.out_specs,
        out_shape=[jax.ShapeDtypeStruct((t, width), BF16)] + ride.out_shapes,
        compiler_params=_params(1),
        name="gmlp_mixer",
    )(x, w_in, ln_g, ln_b, w_s, bs_tile, *ride.args)
    return mix, casts


def _causal_conv(h, prev, cw, cb):
    head = jnp.concatenate([prev, h[:SUBLANES_V7X]], axis=0)

    def shifted(rows):
        top = pltpu.roll(head, rows, 0)[SUBLANES_V7X:]
        return jnp.concatenate([top, pltpu.roll(h, rows, 0)[SUBLANES_V7X:]], axis=0).astype(BF16)

    cw = cw.astype(BF16)
    return (cb.astype(BF16) + cw[0:1] * shifted(2) + cw[1:2] * shifted(1)
            + cw[2:3] * h.astype(BF16))


def _channel_kernel(mix0_ref, x0_ref, mix_ref, x_ref, w_proj_ref, g_mix_ref, b_mix_ref,
                    w_up_ref, cw_ref, cb_ref, w_down_ref, g_ffn_ref, b_ffn_ref, *rest,
                    alpha, fc, tiles_per_seq, n_tiles, n_cast):
    cast_src, (o_ref, *rest) = rest[:n_cast], rest[n_cast:]
    cast_dst, (xmid_ref, xnext_ref, ysum_ref, act_ref, carry_ref) = rest[:n_cast], rest[n_cast:]
    _cast_chunks(cast_src, cast_dst)
    step = pl.program_id(0)
    tm = x_ref.shape[0]
    ffn = w_down_ref.shape[0]

    def mixer_sublayer(mix, x):
        y = jnp.dot(mix, w_proj_ref[...], preferred_element_type=F32)
        return _layer_norm(alpha * x + y, g_mix_ref[...], b_mix_ref[...])

    def final_norm():
        return _layer_norm(ysum_ref[...], g_ffn_ref[...], b_ffn_ref[...])

    @pl.when(step == 0)
    def _():
        xmid_ref[...] = mixer_sublayer(mix0_ref[...], x0_ref[...])
        ysum_ref[...] = jnp.zeros_like(ysum_ref)

    @pl.when(step < n_tiles)
    def _():
        @pl.when(step % tiles_per_seq == 0)
        def _():
            carry_ref[...] = jnp.zeros_like(carry_ref)

        x = xmid_ref[...]
        xb = x.astype(BF16)

        def up_proj(j):
            return tuple(
                jnp.dot(xb, w_up_ref[:, start:start + fc], preferred_element_type=F32)
                for start in (j * fc, ffn + j * fc))

        def conv_cols(h, start):
            cols = slice(start, start + fc)
            prev = carry_ref[:, cols]
            carry_ref[:, cols] = h[tm - SUBLANES_V7X:]
            return _causal_conv(h, prev, cw_ref[:, cols], cb_ref[:, cols])

        n_chunks = ffn // fc
        h_pair = up_proj(0)
        for j in range(n_chunks):
            h_gate, h_val = h_pair
            if j + 1 < n_chunks:
                h_pair = up_proj(j + 1)
            gate = conv_cols(h_gate, j * fc)
            val = conv_cols(h_val, ffn + j * fc)
            act_ref[:, j * fc:(j + 1) * fc] = jax.nn.gelu(gate) * val

        o_ref[...] = final_norm()
        xnext_ref[...] = mixer_sublayer(mix_ref[...], x_ref[...])
        y = jnp.dot(act_ref[...], w_down_ref[...], preferred_element_type=F32)
        ysum_ref[...] = alpha * x + y
        xmid_ref[...] = xnext_ref[...]

    @pl.when(step == n_tiles)
    def _():
        o_ref[...] = final_norm()


def _channel_stage(mix, x, w_proj, ln_mix_g, ln_mix_b, w_up, conv_w, conv_b, w_down,
                   ln_ffn_g, ln_ffn_b, layer, cast_sources, *, alpha, tm, fc, seq):
    t, d = x.shape
    ffn = w_down.shape[0]
    n_tiles = t // tm
    width = mix.shape[1]
    ride = _CastRide(cast_sources, n_tiles, lambda i: jnp.minimum(i, n_tiles - 1))

    def first_tile(cols):
        return pl.BlockSpec((tm, cols), lambda i: (0, 0), pipeline_mode=pl.Buffered(1))

    def next_tile(cols):
        return pl.BlockSpec((tm, cols), lambda i: (jnp.minimum(i + 1, n_tiles - 1), 0))

    def layer_rows(p):
        return _layer_block(p, layer)

    out, *casts = pl.pallas_call(
        functools.partial(_channel_kernel, alpha=alpha, fc=fc, tiles_per_seq=seq // tm,
                          n_tiles=n_tiles, n_cast=ride.n),
        grid=(n_tiles + 1,),
        in_specs=[first_tile(width), first_tile(d), next_tile(width), next_tile(d),
                  _resident(w_proj), layer_rows(ln_mix_g), layer_rows(ln_mix_b),
                  _resident(w_up), layer_rows(conv_w), layer_rows(conv_b), _resident(w_down),
                  layer_rows(ln_ffn_g), layer_rows(ln_ffn_b)] + ride.in_specs,
        out_specs=[pl.BlockSpec((tm, d), lambda i: (jnp.maximum(i - 1, 0), 0))] + ride.out_specs,
        out_shape=[jax.ShapeDtypeStruct((t, d), F32)] + ride.out_shapes,
        scratch_shapes=[
            pltpu.VMEM((tm, d), F32),
            pltpu.VMEM((tm, d), F32),
            pltpu.VMEM((tm, d), F32),
            pltpu.VMEM((tm, ffn), BF16),
            pltpu.VMEM((SUBLANES_V7X, 2 * ffn), F32),
        ],
        compiler_params=_params(1),
        name="channel_stage",
    )(mix, x, mix, x, w_proj, ln_mix_g, ln_mix_b, w_up, conv_w, conv_b, w_down,
      ln_ffn_g, ln_ffn_b, *ride.args)
    return out, casts


def _qkv_kernel(x_ref, wq_ref, wk_ref, wv_ref, *rest, head_dim, tiles_per_seq, n_cast):
    cast_src, (q_ref, k_ref, v_ref, kbar_ref, *cast_dst) = rest[:n_cast], rest[n_cast:]
    _cast_chunks(cast_src, cast_dst)
    q_scale = head_dim ** -0.5 * LOG2_E
    tm = x_ref.shape[0]
    tn = wq_ref.shape[1]
    n_pairs = tn // LANES_V7X
    n_heads = tn // head_dim
    tile_pos = (pl.program_id(0) % tiles_per_seq) * tm
    sub = 2 * MOBA_BLOCK
    lane = lax.broadcasted_iota(jnp.int32, (MOBA_BLOCK, LANES_V7X), 1)

    def project(r):
        xb = x_ref[r * sub:(r + 1) * sub, :].astype(BF16)
        return tuple(jnp.dot(xb, w_ref[...], preferred_element_type=F32)
                     for w_ref in (wq_ref, wk_ref, wv_ref))

    kbar = []
    n_sub = tm // sub
    projected = project(0)
    for r in range(n_sub):
        rows = slice(r * sub, (r + 1) * sub)
        q, k, v = projected
        if r + 1 < n_sub:
            projected = project(r + 1)
        q = q * q_scale
        first_block = (tile_pos + r * sub) // MOBA_BLOCK
        block_onehot = jnp.concatenate(
            [jnp.where(lane == first_block + b, 1.0, 0.0).astype(BF16)
             for b in range(sub // MOBA_BLOCK)], axis=0)
        q_t = q.T.astype(BF16)
        v_t = v.T.astype(BF16)
        for p in range(n_pairs):
            cols = slice(p * LANES_V7X, (p + 1) * LANES_V7X)
            q_ref[0, p, :, rows] = q_t[cols]
            k_ref[0, p, rows, :LANES_V7X] = k[:, cols].astype(BF16)
            k_ref[0, p, rows, LANES_V7X:] = block_onehot
        for h in range(n_heads):
            v_ref[0, h, :head_dim, rows] = v_t[h * head_dim:(h + 1) * head_dim]
            v_ref[0, h, head_dim:, rows] = jnp.ones((v_ref.shape[2] - head_dim, sub), BF16)
        kbar.append(jnp.mean(k.reshape(sub // MOBA_BLOCK, MOBA_BLOCK, tn), axis=1))
    kbar_ref[0] = jnp.concatenate(kbar, axis=0)


def _qkv_proj(x, w_qkv, cast_sources, *, batch, seq, tm, tn):
    t, d = x.shape
    n_pairs = d // LANES_V7X
    pairs_per_step = tn // LANES_V7X
    tiles_per_seq = seq // tm
    n_col = d // tn
    head_dim = d // N_HEADS
    heads_per_step = tn // head_dim
    v_rows = head_dim + BF16_SUBLANES_V7X

    def row_major_map(i, j):
        return (i // tiles_per_seq, j, i % tiles_per_seq, 0)

    def feature_major_map(i, j):
        return (i // tiles_per_seq, j, 0, i % tiles_per_seq)

    def w_cols(which):
        return pl.BlockSpec((d, tn), lambda i, j: (0, which * n_col + j))

    n_rows = t // tm
    ride = _CastRide(cast_sources, n_rows * n_col, lambda i, j: i * n_col + j)
    q_t, kaug, v_t, kbar, *casts = pl.pallas_call(
        functools.partial(_qkv_kernel, head_dim=head_dim, tiles_per_seq=tiles_per_seq,
                          n_cast=ride.n),
        grid=(n_rows, n_col),
        in_specs=[pl.BlockSpec((tm, d), lambda i, j: (i, 0)), w_cols(0), w_cols(1), w_cols(2)]
        + ride.in_specs,
        out_specs=[
            pl.BlockSpec((1, pairs_per_step, LANES_V7X, tm), feature_major_map),
            pl.BlockSpec((1, pairs_per_step, tm, 2 * LANES_V7X), row_major_map),
            pl.BlockSpec((1, heads_per_step, v_rows, tm), feature_major_map),
            pl.BlockSpec((1, tm // MOBA_BLOCK, tn),
                         lambda i, j: (i // tiles_per_seq, i % tiles_per_seq, j)),
        ] + ride.out_specs,
        out_shape=[
            jax.ShapeDtypeStruct((batch, n_pairs, LANES_V7X, seq), BF16),
            jax.ShapeDtypeStruct((batch, n_pairs, seq, 2 * LANES_V7X), BF16),
            jax.ShapeDtypeStruct((batch, N_HEADS, v_rows, seq), BF16),
            jax.ShapeDtypeStruct((batch, seq // MOBA_BLOCK, d), F32),
        ] + ride.out_shapes,
        compiler_params=_params(2),
        name="moba_qkv_proj",
    )(x, w_qkv, w_qkv, w_qkv, *ride.args)
    return (q_t, kaug, v_t, kbar), casts


def _moba_attn_kernel(q_ref, k_ref, v_ref, kbar_ref, tab_ref, o_ref, s_ref, *, head_dim):
    blk_len = MOBA_BLOCK
    tq = pair_len = 2 * blk_len
    width = 2 * tq
    n_tiles = q_ref.shape[3] // tq

    def slots(tile):
        return 0 if tile % 2 == 0 else n_tiles - 1

    def score_unit(tile, pair, q_aug, cmax):
        s = jnp.dot(k_ref[0, 0, pair * pair_len:(pair + 1) * pair_len, :], q_aug,
                    preferred_element_type=F32)
        if pair == tile:
            s = s + tab_ref[0, blk_len:]
        elif pair == tile - 1:
            s = jnp.concatenate([s[:blk_len], s[blk_len:] + tab_ref[0, :blk_len]], axis=0)
        s_ref[slots(tile) + pair] = s
        folded = jnp.max(s.reshape(pair_len // SUBLANES_V7X, SUBLANES_V7X, width), axis=0)
        return folded if cmax is None else jnp.maximum(cmax, folded)

    def value_unit(tile, first, count, col_m, acc):
        keys = slice(first * pair_len, (first + count) * pair_len)
        base = slots(tile) + first
        out = []
        for h in range(2):
            cols = slice(h * tq, (h + 1) * tq)
            s = s_ref[base:base + count, :, cols].reshape(count * pair_len, tq)
            p = jnp.exp2((s - col_m[:, cols]).astype(BF16))
            pv = jnp.dot(v_ref[0, h, :, keys], p, preferred_element_type=F32)
            out.append(pv if acc is None else acc[h] + pv)
        return out

    def finish(tile, acc):
        heads = [a[:head_dim] / a[head_dim:head_dim + 1] for a in acc]
        o_ref[0, tile * tq:(tile + 1) * tq, :] = (
            jnp.concatenate(heads, axis=0).T.astype(o_ref.dtype))

    col_m = None
    for step in range(n_tiles + 1):
        scoring, valuing = step, step - 1
        score_pairs = list(range(scoring + 1)) if scoring < n_tiles else []
        value_groups = ([(first, min(2, valuing + 1 - first)) for first in range(0, valuing + 1, 2)]
                        if valuing >= 0 else [])
        q_aug = _tile_queries(scoring, q_ref, kbar_ref, head_dim) if score_pairs else None
        cmax = acc = None
        while score_pairs or value_groups:
            for pair in score_pairs[:2]:
                cmax = score_unit(scoring, pair, q_aug, cmax)
            score_pairs = score_pairs[2:]
            if value_groups:
                acc = value_unit(valuing, *value_groups.pop(0), col_m, acc)
        if valuing >= 0:
            finish(valuing, acc)
        if cmax is not None:
            col_m = jnp.max(cmax, axis=0, keepdims=True)


def _tile_queries(tile, q_ref, kbar_ref, head_dim):
    blk_len = MOBA_BLOCK
    tq = 2 * blk_len
    q_t = q_ref[0, 0, :, tile * tq:(tile + 1) * tq]
    n_blocks = kbar_ref.shape[1]
    feat_row = lax.broadcasted_iota(jnp.int32, q_t.shape, 0)
    kbar = kbar_ref[0].astype(BF16)
    blk = lax.broadcasted_iota(jnp.int32, (n_blocks, tq), 0)
    query = lax.broadcasted_iota(jnp.int32, (n_blocks, tq), 1)
    cur = 2 * tile + jnp.where(query >= blk_len, 1, 0)

    q_aug = []
    for h in range(2):
        in_head = (feat_row >= head_dim) if h else (feat_row < head_dim)
        qm = jnp.where(in_head, q_t, jnp.zeros_like(q_t))
        gate = jnp.dot(kbar, qm, preferred_element_type=F32)
        gate = jnp.where(blk < cur, gate, -jnp.inf)
        rank = jnp.zeros(gate.shape, F32)
        for m in range(min(n_blocks, 2 * tile + 1)):
            gm = gate[m:m + 1, :]
            tie = jnp.where(blk > m, 1.0, 0.0)
            rank = rank + jnp.where(gm > gate, 1.0, jnp.where(gm == gate, tie, 0.0))
        feat = jnp.where(blk < cur,
                         jnp.where(rank < MOBA_TOPK, 0.0, MASK_NEG),
                         jnp.where(blk == cur, 0.0, MASK_NEG))
        feat = jnp.concatenate(
            [feat, jnp.zeros((LANES_V7X - n_blocks, tq), F32)], axis=0)
        q_aug.append(jnp.concatenate([qm, feat.astype(BF16)], axis=0))
    return jnp.concatenate(q_aug, axis=1)


def _rel_bucket(dist):
    n = jnp.maximum(dist, 0)
    max_exact = REL_BUCKETS // 2
    nf = jnp.maximum(n, 1).astype(F32)
    large = max_exact + (jnp.log(nf / max_exact) / math.log(REL_MAX_DIST / max_exact)
                         * (REL_BUCKETS - max_exact)).astype(jnp.int32)
    large = jnp.minimum(large, REL_BUCKETS - 1)
    return jnp.where(n < max_exact, n, large)


def _bias_tables(rel_bias):
    blk = MOBA_BLOCK
    n_keys, n_q = 3 * blk, 2 * blk
    n_heads = rel_bias.shape[1]
    bias_h = rel_bias.T
    d_min = blk - (n_keys - 1)
    dist = jnp.arange(d_min, blk + n_q)
    by_dist = (bias_h[:, _rel_bucket(dist)] - bias_h[:, REL_BUCKETS - 1:]) * LOG2_E
    by_dist = jnp.where(dist[None] < 0, MASK_NEG, by_dist)
    period = by_dist.shape[1] + 1
    assert period % LANES_V7X == 0 and (period - n_q) % LANES_V7X == 0
    rows = jnp.pad(by_dist, ((0, 0), (0, 1)))[:, None, :]
    shift = (period - n_q) - (blk - d_min)

    def expand(rows_ref, out_ref):
        for h in range(2):
            wide = jnp.broadcast_to(rows_ref[h], (n_keys, period))
            rolled = pltpu.roll(wide, shift, 1, stride=1, stride_axis=0)
            out_ref[0, :, h * n_q:(h + 1) * n_q] = rolled[:, period - n_q:]

    return pl.pallas_call(
        expand,
        grid=(n_heads // 2,),
        in_specs=[pl.BlockSpec((2, 1, period), lambda p: (p, 0, 0))],
        out_specs=pl.BlockSpec((1, n_keys, 2 * n_q), lambda p: (p, 0, 0)),
        out_shape=jax.ShapeDtypeStruct((n_heads // 2, n_keys, 2 * n_q), F32),
        compiler_params=_params(1),
        name="rel_bias_tables",
    )(rows)


def _moba_attention(q_t, kaug, v_t, kbar, tables, *, batch, seq):
    n_pairs = q_t.shape[1]
    d = n_pairs * LANES_V7X
    tq = 2 * MOBA_BLOCK
    n_blocks = seq // MOBA_BLOCK
    v_rows = v_t.shape[2]
    return pl.pallas_call(
        functools.partial(_moba_attn_kernel, head_dim=d // N_HEADS),
        grid=(n_pairs, batch),
        in_specs=[
            pl.BlockSpec((1, 1, LANES_V7X, seq), lambda p, b: (b, p, 0, 0)),
            pl.BlockSpec((1, 1, seq, 2 * LANES_V7X), lambda p, b: (b, p, 0, 0)),
            pl.BlockSpec((1, 2, v_rows, seq), lambda p, b: (b, p, 0, 0)),
            pl.BlockSpec((1, n_blocks, LANES_V7X), lambda p, b: (b, 0, p)),
            pl.BlockSpec((1,) + tables.shape[1:], lambda p, b: (p, 0, 0),
                         pipeline_mode=pl.Buffered(1)),
        ],
        out_specs=pl.BlockSpec((1, seq, LANES_V7X), lambda p, b: (b, 0, p)),
        out_shape=jax.ShapeDtypeStruct((batch, seq, d), BF16),
        scratch_shapes=[
            pltpu.VMEM((2 * (seq // tq) - 1, tq, 2 * tq), F32),
        ],
        compiler_params=_params(2),
        name="moba_attention",
    )(q_t, kaug, v_t, kbar, tables)


def _tiles(seq):
    assert seq % (SUBLANES_V7X * MOBA_BLOCK) == 0, "qkv tile must hold 8 whole key blocks"
    return dict(tm_mix=min(seq, 512), tm_ffn=min(seq, 512), tm_qkv=SUBLANES_V7X * MOBA_BLOCK,
                tn_qkv=2 * LANES_V7X, fc_ffn=2 * LANES_V7X)


def kernel(x, ln_mix_g, ln_mix_b, ln_ffn_g, ln_ffn_b, a_w_in, a_ln_g, a_ln_b, a_w_s, a_b_s,
           a_w_out, b_w_qkv, b_w_o, rel_bias, f_w_up, f_conv_w, f_conv_b, f_w_down):
    batch, seq, d = x.shape
    depth = ln_mix_g.shape[0]
    alpha = (2 * depth) ** 0.25
    cfg = _tiles(seq)
    width = a_w_out.shape[1]
    n_groups = a_w_s.shape[1]

    def rows(p):
        return p[:, None, :]

    ln_mix_g, ln_mix_b, ln_ffn_g, ln_ffn_b, a_ln_g, a_ln_b, f_conv_b = (
        rows(p) for p in (ln_mix_g, ln_mix_b, ln_ffn_g, ln_ffn_b, a_ln_g, a_ln_b, f_conv_b))
    bs_tile = jnp.repeat(a_b_s.transpose(0, 2, 1), width // n_groups, axis=2)
    tables = _bias_tables(rel_bias)

    def layer_sources(i):
        mixer = (a_w_in, i // 2) if i % 2 == 0 else (b_w_qkv, i // 2)
        proj = (a_w_out, i // 2) if i % 2 == 0 else (b_w_o, i // 2)
        return [mixer, proj, (f_w_up, i), (f_w_down, i)]

    mixer_w = a_w_in[0].astype(BF16)
    channel_w = None
    h = x.reshape(batch * seq, d)
    for i in range(depth):
        j = i // 2
        first_casts = layer_sources(0)[1:] if i == 0 else []
        if i % 2 == 0:
            mix, casts = _gmlp_mixer(h, mixer_w, a_ln_g, a_ln_b, a_w_s, bs_tile, j, first_casts,
                                     tm=cfg["tm_mix"])
        else:
            qkv, casts = _qkv_proj(h, mixer_w, first_casts, batch=batch, seq=seq,
                                   tm=cfg["tm_qkv"], tn=cfg["tn_qkv"])
            mix = _moba_attention(*qkv, tables, batch=batch, seq=seq).reshape(batch * seq, d)
        w_proj, w_up, w_down = casts or channel_w
        next_sources = layer_sources(i + 1) if i + 1 < depth else []
        h, next_w = _channel_stage(mix, h, w_proj, ln_mix_g, ln_mix_b, w_up, f_conv_w, f_conv_b,
                                   w_down, ln_ffn_g, ln_ffn_b, i, next_sources, alpha=alpha,
                                   tm=cfg["tm_ffn"], fc=cfg["fc_ffn"], seq=seq)
        if next_w:
            mixer_w, *channel_w = next_w
    return h.reshape(batch, seq, d)
```

```python
import functools
import math

import jax
import jax.numpy as jnp
from jax import lax
from jax.experimental import pallas as pl
from jax.experimental.pallas import tpu as pltpu

F32 = jnp.float32
BF16 = jnp.bfloat16

GMLP_CHUNK = 128
N_HEADS = 16
MOBA_BLOCK = 256
MOBA_TOPK = 3
REL_BUCKETS = 32
REL_MAX_DIST = 128
LN_EPS = 1e-5

LANES_V7X = 128
SUBLANES_V7X = 8
BF16_SUBLANES_V7X = 16
VMEM_LIMIT_BYTES_V7X = 56 * 1024 * 1024

MASK_NEG = -1e30
LOG2_E = math.log2(math.e)


def _layer_norm(x, g, b):
    mu = jnp.mean(x, axis=-1, keepdims=True)
    xc = x - mu
    var = jnp.mean(xc * xc, axis=-1, keepdims=True)
    return xc * lax.rsqrt(var + LN_EPS) * g + b


def _layer_block(stacked, layer):
    tail = stacked.shape[1:]
    return pl.BlockSpec((None,) + tail, lambda *_: (layer,) + (0,) * len(tail),
                        pipeline_mode=pl.Buffered(1))


def _resident(array):
    return pl.BlockSpec(array.shape, lambda *_: (0,) * array.ndim, pipeline_mode=pl.Buffered(1))


def _params(n_axes):
    return pltpu.CompilerParams(
        dimension_semantics=("arbitrary",) * n_axes,
        vmem_limit_bytes=VMEM_LIMIT_BYTES_V7X,
    )


class _CastRide:
    def __init__(self, sources, n_chunks, chunk_of):
        self.args, self.in_specs, self.out_specs, self.out_shapes = [], [], [], []
        for stacked, layer in sources:
            _, r, c = stacked.shape
            share = next(s for s in range(1, n_chunks + 1)
                         if n_chunks % s == 0 and r % (n_chunks // s * BF16_SUBLANES_V7X) == 0)
            rows = r // (n_chunks // share)
            self.args.append(stacked)
            self.in_specs.append(pl.BlockSpec(
                (None, rows, c),
                lambda *g, layer=layer, share=share: (layer, chunk_of(*g) // share, 0)))
            self.out_specs.append(pl.BlockSpec(
                (rows, c), lambda *g, share=share: (chunk_of(*g) // share, 0)))
            self.out_shapes.append(jax.ShapeDtypeStruct((r, c), BF16))
        self.n = len(self.args)


def _cast_chunks(src_refs, dst_refs):
    for src, dst in zip(src_refs, dst_refs, strict=True):
        dst[...] = src[...].astype(dst.dtype)


def _gmlp_kernel(x_ref, w_in_ref, lng_ref, lnb_ref, ws_ref, bs_ref, *rest, n_groups, n_cast):
    cast_src, (o_ref, *cast_dst) = rest[:n_cast], rest[n_cast:]
    _cast_chunks(cast_src, cast_dst)
    x = x_ref[...]
    tm = x.shape[0]
    width = o_ref.shape[1]
    gdim = width // n_groups
    z = jnp.dot(x.astype(BF16), w_in_ref[...], preferred_element_type=F32)
    z = jax.nn.gelu(z)
    u = z[:, :width]
    v = _layer_norm(z[:, width:], lng_ref[...], lnb_ref[...]).astype(BF16)
    row = lax.broadcasted_iota(jnp.int32, (GMLP_CHUNK, GMLP_CHUNK), 0)
    col = lax.broadcasted_iota(jnp.int32, (GMLP_CHUNK, GMLP_CHUNK), 1)
    causal = col <= row
    w_masked = [jnp.where(causal, ws_ref[g], 0.0).astype(BF16) for g in range(n_groups)]
    chunks = []
    for c in range(tm // GMLP_CHUNK):
        rows = slice(c * GMLP_CHUNK, (c + 1) * GMLP_CHUNK)
        groups = [
            jnp.dot(w_masked[g], v[rows, g * gdim:(g + 1) * gdim], preferred_element_type=F32)
            for g in range(n_groups)
        ]
        chunks.append(jnp.concatenate(groups, axis=1) + bs_ref[...])
    sv = jnp.concatenate(chunks, axis=0)
    o_ref[...] = (u * sv).astype(o_ref.dtype)


def _gmlp_mixer(x, w_in, ln_g, ln_b, w_s, bs_tile, layer, cast_sources, *, tm):
    t, d = x.shape
    width = w_in.shape[1] // 2
    n_steps = t // tm
    ride = _CastRide(cast_sources, n_steps, lambda i: i)
    mix, *casts = pl.pallas_call(
        functools.partial(_gmlp_kernel, n_groups=w_s.shape[1], n_cast=ride.n),
        grid=(n_steps,),
        in_specs=[pl.BlockSpec((tm, d), lambda i: (i, 0)), _resident(w_in)]
        + [_layer_block(a, layer) for a in (ln_g, ln_b, w_s, bs_tile)] + ride.in_specs,
        out_specs=[pl.BlockSpec((tm, width), lambda i: (i, 0))] + ride.out_specs,
        out_shape=[jax.ShapeDtypeStruct((t, width), BF16)] + ride.out_shapes,
        compiler_params=_params(1),
        name="gmlp_mixer",
    )(x, w_in, ln_g, ln_b, w_s, bs_tile, *ride.args)
    return mix, casts


def _causal_conv(h, prev, cw, cb):
    head = jnp.concatenate([prev, h[:SUBLANES_V7X]], axis=0)

    def shifted(rows):
        top = pltpu.roll(head, rows, 0)[SUBLANES_V7X:]
        return jnp.concatenate([top, pltpu.roll(h, rows, 0)[SUBLANES_V7X:]], axis=0).astype(BF16)

    cw = cw.astype(BF16)
    return (cb.astype(BF16) + cw[0:1] * shifted(2) + cw[1:2] * shifted(1)
            + cw[2:3] * h.astype(BF16))


def _channel_kernel(mix0_ref, x0_ref, mix_ref, x_ref, w_proj_ref, g_mix_ref, b_mix_ref,
                    w_up_ref, cw_ref, cb_ref, w_down_ref, g_ffn_ref, b_ffn_ref, *rest,
                    alpha, fc, tiles_per_seq, n_tiles, n_cast):
    cast_src, (o_ref, *rest) = rest[:n_cast], rest[n_cast:]
    cast_dst, (xmid_ref, xnext_ref, ysum_ref, act_ref, carry_ref) = rest[:n_cast], rest[n_cast:]
    _cast_chunks(cast_src, cast_dst)
    step = pl.program_id(0)
    tm = x_ref.shape[0]
    ffn = w_down_ref.shape[0]

    def mixer_sublayer(mix, x):
        y = jnp.dot(mix, w_proj_ref[...], preferred_element_type=F32)
        return _layer_norm(alpha * x + y, g_mix_ref[...], b_mix_ref[...])

    def final_norm():
        return _layer_norm(ysum_ref[...], g_ffn_ref[...], b_ffn_ref[...])

    @pl.when(step == 0)
    def _():
        xmid_ref[...] = mixer_sublayer(mix0_ref[...], x0_ref[...])
        ysum_ref[...] = jnp.zeros_like(ysum_ref)

    @pl.when(step < n_tiles)
    def _():
        @pl.when(step % tiles_per_seq == 0)
        def _():
            carry_ref[...] = jnp.zeros_like(carry_ref)

        x = xmid_ref[...]
        xb = x.astype(BF16)

        def up_proj(j):
            return tuple(
                jnp.dot(xb, w_up_ref[:, start:start + fc], preferred_element_type=F32)
                for start in (j * fc, ffn + j * fc))

        def conv_cols(h, start):
            cols = slice(start, start + fc)
            prev = carry_ref[:, cols]
            carry_ref[:, cols] = h[tm - SUBLANES_V7X:]
            return _causal_conv(h, prev, cw_ref[:, cols], cb_ref[:, cols])

        n_chunks = ffn // fc
        h_pair = up_proj(0)
        for j in range(n_chunks):
            h_gate, h_val = h_pair
            if j + 1 < n_chunks:
                h_pair = up_proj(j + 1)
            gate = conv_cols(h_gate, j * fc)
            val = conv_cols(h_val, ffn + j * fc)
            act_ref[:, j * fc:(j + 1) * fc] = jax.nn.gelu(gate) * val

        o_ref[...] = final_norm()
        xnext_ref[...] = mixer_sublayer(mix_ref[...], x_ref[...])
        y = jnp.dot(act_ref[...], w_down_ref[...], preferred_element_type=F32)
        ysum_ref[...] = alpha * x + y
        xmid_ref[...] = xnext_ref[...]

    @pl.when(step == n_tiles)
    def _():
        o_ref[...] = final_norm()


def _channel_stage(mix, x, w_proj, ln_mix_g, ln_mix_b, w_up, conv_w, conv_b, w_down,
                   ln_ffn_g, ln_ffn_b, layer, cast_sources, *, alpha, tm, fc, seq):
    t, d = x.shape
    ffn = w_down.shape[0]
    n_tiles = t // tm
    width = mix.shape[1]
    ride = _CastRide(cast_sources, n_tiles, lambda i: jnp.minimum(i, n_tiles - 1))

    def first_tile(cols):
        return pl.BlockSpec((tm, cols), lambda i: (0, 0), pipeline_mode=pl.Buffered(1))

    def next_tile(cols):
        return pl.BlockSpec((tm, cols), lambda i: (jnp.minimum(i + 1, n_tiles - 1), 0))

    def layer_rows(p):
        return _layer_block(p, layer)

    out, *casts = pl.pallas_call(
        functools.partial(_channel_kernel, alpha=alpha, fc=fc, tiles_per_seq=seq // tm,
                          n_tiles=n_tiles, n_cast=ride.n),
        grid=(n_tiles + 1,),
        in_specs=[first_tile(width), first_tile(d), next_tile(width), next_tile(d),
                  _resident(w_proj), layer_rows(ln_mix_g), layer_rows(ln_mix_b),
                  _resident(w_up), layer_rows(conv_w), layer_rows(conv_b), _resident(w_down),
                  layer_rows(ln_ffn_g), layer_rows(ln_ffn_b)] + ride.in_specs,
        out_specs=[pl.BlockSpec((tm, d), lambda i: (jnp.maximum(i - 1, 0), 0))] + ride.out_specs,
        out_shape=[jax.ShapeDtypeStruct((t, d), F32)] + ride.out_shapes,
        scratch_shapes=[
            pltpu.VMEM((tm, d), F32),
            pltpu.VMEM((tm, d), F32),
            pltpu.VMEM((tm, d), F32),
            pltpu.VMEM((tm, ffn), BF16),
            pltpu.VMEM((SUBLANES_V7X, 2 * ffn), F32),
        ],
        compiler_params=_params(1),
        name="channel_stage",
    )(mix, x, mix, x, w_proj, ln_mix_g, ln_mix_b, w_up, conv_w, conv_b, w_down,
      ln_ffn_g, ln_ffn_b, *ride.args)
    return out, casts


def _qkv_kernel(x_ref, wq_ref, wk_ref, wv_ref, *rest, head_dim, tiles_per_seq, n_cast):
    cast_src, (q_ref, k_ref, v_ref, kbar_ref, *cast_dst) = rest[:n_cast], rest[n_cast:]
    _cast_chunks(cast_src, cast_dst)
    q_scale = head_dim ** -0.5 * LOG2_E
    tm = x_ref.shape[0]
    tn = wq_ref.shape[1]
    n_pairs = tn // LANES_V7X
    n_heads = tn // head_dim
    tile_pos = (pl.program_id(0) % tiles_per_seq) * tm
    sub = 2 * MOBA_BLOCK
    lane = lax.broadcasted_iota(jnp.int32, (MOBA_BLOCK, LANES_V7X), 1)

    def project(r):
        xb = x_ref[r * sub:(r + 1) * sub, :].astype(BF16)
        return tuple(jnp.dot(xb, w_ref[...], preferred_element_type=F32)
                     for w_ref in (wq_ref, wk_ref, wv_ref))

    kbar = []
    n_sub = tm // sub
    projected = project(0)
    for r in range(n_sub):
        rows = slice(r * sub, (r + 1) * sub)
        q, k, v = projected
        if r + 1 < n_sub:
            projected = project(r + 1)
        q = q * q_scale
        first_block = (tile_pos + r * sub) // MOBA_BLOCK
        block_onehot = jnp.concatenate(
            [jnp.where(lane == first_block + b, 1.0, 0.0).astype(BF16)
             for b in range(sub // MOBA_BLOCK)], axis=0)
        q_t = q.T.astype(BF16)
        v_t = v.T.astype(BF16)
        for p in range(n_pairs):
            cols = slice(p * LANES_V7X, (p + 1) * LANES_V7X)
            q_ref[0, p, :, rows] = q_t[cols]
            k_ref[0, p, rows, :LANES_V7X] = k[:, cols].astype(BF16)
            k_ref[0, p, rows, LANES_V7X:] = block_onehot
        for h in range(n_heads):
            v_ref[0, h, :head_dim, rows] = v_t[h * head_dim:(h + 1) * head_dim]
            v_ref[0, h, head_dim:, rows] = jnp.ones((v_ref.shape[2] - head_dim, sub), BF16)
        kbar.append(jnp.mean(k.reshape(sub // MOBA_BLOCK, MOBA_BLOCK, tn), axis=1))
    kbar_ref[0] = jnp.concatenate(kbar, axis=0)


def _qkv_proj(x, w_qkv, cast_sources, *, batch, seq, tm, tn):
    t, d = x.shape
    n_pairs = d // LANES_V7X
    pairs_per_step = tn // LANES_V7X
    tiles_per_seq = seq // tm
    n_col = d // tn
    head_dim = d // N_HEADS
    heads_per_step = tn // head_dim
    v_rows = head_dim + BF16_SUBLANES_V7X

    def row_major_map(i, j):
        return (i // tiles_per_seq, j, i % tiles_per_seq, 0)

    def feature_major_map(i, j):
        return (i // tiles_per_seq, j, 0, i % tiles_per_seq)

    def w_cols(which):
        return pl.BlockSpec((d, tn), lambda i, j: (0, which * n_col + j))

    n_rows = t // tm
    ride = _CastRide(cast_sources, n_rows * n_col, lambda i, j: i * n_col + j)
    q_t, kaug, v_t, kbar, *casts = pl.pallas_call(
        functools.partial(_qkv_kernel, head_dim=head_dim, tiles_per_seq=tiles_per_seq,
                          n_cast=ride.n),
        grid=(n_rows, n_col),
        in_specs=[pl.BlockSpec((tm, d), lambda i, j: (i, 0)), w_cols(0), w_cols(1), w_cols(2)]
        + ride.in_specs,
        out_specs=[
            pl.BlockSpec((1, pairs_per_step, LANES_V7X, tm), feature_major_map),
            pl.BlockSpec((1, pairs_per_step, tm, 2 * LANES_V7X), row_major_map),
            pl.BlockSpec((1, heads_per_step, v_rows, tm), feature_major_map),
            pl.BlockSpec((1, tm // MOBA_BLOCK, tn),
                         lambda i, j: (i // tiles_per_seq, i % tiles_per_seq, j)),
        ] + ride.out_specs,
        out_shape=[
            jax.ShapeDtypeStruct((batch, n_pairs, LANES_V7X, seq), BF16),
            jax.ShapeDtypeStruct((batch, n_pairs, seq, 2 * LANES_V7X), BF16),
            jax.ShapeDtypeStruct((batch, N_HEADS, v_rows, seq), BF16),
            jax.ShapeDtypeStruct((batch, seq // MOBA_BLOCK, d), F32),
        ] + ride.out_shapes,
        compiler_params=_params(2),
        name="moba_qkv_proj",
    )(x, w_qkv, w_qkv, w_qkv, *ride.args)
    return (q_t, kaug, v_t, kbar), casts


def _moba_attn_kernel(q_ref, k_ref, v_ref, kbar_ref, tab_ref, o_ref, s_ref, *, head_dim):
    blk_len = MOBA_BLOCK
    tq = pair_len = 2 * blk_len
    width = 2 * tq
    n_tiles = q_ref.shape[3] // tq

    def slots(tile):
        return 0 if tile % 2 == 0 else n_tiles - 1

    def score_unit(tile, pair, q_aug, cmax):
        s = jnp.dot(k_ref[0, 0, pair * pair_len:(pair + 1) * pair_len, :], q_aug,
                    preferred_element_type=F32)
        if pair == tile:
            s = s + tab_ref[0, blk_len:]
        elif pair == tile - 1:
            s = jnp.concatenate([s[:blk_len], s[blk_len:] + tab_ref[0, :blk_len]], axis=0)
        s_ref[slots(tile) + pair] = s
        folded = jnp.max(s.reshape(pair_len // SUBLANES_V7X, SUBLANES_V7X, width), axis=0)
        return folded if cmax is None else jnp.maximum(cmax, folded)

    def value_unit(tile, first, count, col_m, acc):
        keys = slice(first * pair_len, (first + count) * pair_len)
        base = slots(tile) + first
        out = []
        for h in range(2):
            cols = slice(h * tq, (h + 1) * tq)
            s = s_ref[base:base + count, :, cols].reshape(count * pair_len, tq)
            p = jnp.exp2((s - col_m[:, cols]).astype(BF16))
            pv = jnp.dot(v_ref[0, h, :, keys], p, preferred_element_type=F32)
            out.append(pv if acc is None else acc[h] + pv)
        return out

    def finish(tile, acc):
        heads = [a[:head_dim] / a[head_dim:head_dim + 1] for a in acc]
        o_ref[0, tile * tq:(tile + 1) * tq, :] = (
            jnp.concatenate(heads, axis=0).T.astype(o_ref.dtype))

    col_m = None
    for step in range(n_tiles + 1):
        scoring, valuing = step, step - 1
        score_pairs = list(range(scoring + 1)) if scoring < n_tiles else []
        value_groups = [(pair, 1) for pair in range(valuing + 1)]
        q_aug = _tile_queries(scoring, q_ref, kbar_ref, head_dim) if score_pairs else None
        cmax = acc = None
        while score_pairs or value_groups:
            for pair in score_pairs[:1]:
                cmax = score_unit(scoring, pair, q_aug, cmax)
            score_pairs = score_pairs[1:]
            if value_groups:
                acc = value_unit(valuing, *value_groups.pop(0), col_m, acc)
        if valuing >= 0:
            finish(valuing, acc)
        if cmax is not None:
            col_m = jnp.max(cmax, axis=0, keepdims=True)


def _tile_queries(tile, q_ref, kbar_ref, head_dim):
    blk_len = MOBA_BLOCK
    tq = 2 * blk_len
    q_t = q_ref[0, 0, :, tile * tq:(tile + 1) * tq]
    n_blocks = kbar_ref.shape[1]
    feat_row = lax.broadcasted_iota(jnp.int32, q_t.shape, 0)
    kbar = kbar_ref[0].astype(BF16)
    blk = lax.broadcasted_iota(jnp.int32, (n_blocks, tq), 0)
    query = lax.broadcasted_iota(jnp.int32, (n_blocks, tq), 1)
    cur = 2 * tile + jnp.where(query >= blk_len, 1, 0)

    q_aug = []
    for h in range(2):
        in_head = (feat_row >= head_dim) if h else (feat_row < head_dim)
        qm = jnp.where(in_head, q_t, jnp.zeros_like(q_t))
        gate = jnp.dot(kbar, qm, preferred_element_type=F32)
        gate = jnp.where(blk < cur, gate, -jnp.inf)
        rank = jnp.zeros(gate.shape, F32)
        for m in range(min(n_blocks, 2 * tile + 1)):
            gm = gate[m:m + 1, :]
            tie = jnp.where(blk > m, 1.0, 0.0)
            rank = rank + jnp.where(gm > gate, 1.0, jnp.where(gm == gate, tie, 0.0))
        feat = jnp.where(blk < cur,
                         jnp.where(rank < MOBA_TOPK, 0.0, MASK_NEG),
                         jnp.where(blk == cur, 0.0, MASK_NEG))
        feat = jnp.concatenate(
            [feat, jnp.zeros((LANES_V7X - n_blocks, tq), F32)], axis=0)
        q_aug.append(jnp.concatenate([qm, feat.astype(BF16)], axis=0))
    return jnp.concatenate(q_aug, axis=1)


def _rel_bucket(dist):
    n = jnp.maximum(dist, 0)
    max_exact = REL_BUCKETS // 2
    nf = jnp.maximum(n, 1).astype(F32)
    large = max_exact + (jnp.log(nf / max_exact) / math.log(REL_MAX_DIST / max_exact)
                         * (REL_BUCKETS - max_exact)).astype(jnp.int32)
    large = jnp.minimum(large, REL_BUCKETS - 1)
    return jnp.where(n < max_exact, n, large)


def _bias_tables(rel_bias):
    blk = MOBA_BLOCK
    n_keys, n_q = 3 * blk, 2 * blk
    n_heads = rel_bias.shape[1]
    bias_h = rel_bias.T
    d_min = blk - (n_keys - 1)
    dist = jnp.arange(d_min, blk + n_q)
    by_dist = (bias_h[:, _rel_bucket(dist)] - bias_h[:, REL_BUCKETS - 1:]) * LOG2_E
    by_dist = jnp.where(dist[None] < 0, MASK_NEG, by_dist)
    period = by_dist.shape[1] + 1
    assert period % LANES_V7X == 0 and (period - n_q) % LANES_V7X == 0
    rows = jnp.pad(by_dist, ((0, 0), (0, 1)))[:, None, :]
    shift = (period - n_q) - (blk - d_min)

    def expand(rows_ref, out_ref):
        for h in range(2):
            wide = jnp.broadcast_to(rows_ref[h], (n_keys, period))
            rolled = pltpu.roll(wide, shift, 1, stride=1, stride_axis=0)
            out_ref[0, :, h * n_q:(h + 1) * n_q] = rolled[:, period - n_q:]

    return pl.pallas_call(
        expand,
        grid=(n_heads // 2,),
        in_specs=[pl.BlockSpec((2, 1, period), lambda p: (p, 0, 0))],
        out_specs=pl.BlockSpec((1, n_keys, 2 * n_q), lambda p: (p, 0, 0)),
        out_shape=jax.ShapeDtypeStruct((n_heads // 2, n_keys, 2 * n_q), F32),
        compiler_params=_params(1),
        name="rel_bias_tables",
    )(rows)


def _moba_attention(q_t, kaug, v_t, kbar, tables, *, batch, seq):
    n_pairs = q_t.shape[1]
    d = n_pairs * LANES_V7X
    tq = 2 * MOBA_BLOCK
    n_blocks = seq // MOBA_BLOCK
    v_rows = v_t.shape[2]
    return pl.pallas_call(
        functools.partial(_moba_attn_kernel, head_dim=d // N_HEADS),
        grid=(n_pairs, batch),
        in_specs=[
            pl.BlockSpec((1, 1, LANES_V7X, seq), lambda p, b: (b, p, 0, 0)),
            pl.BlockSpec((1, 1, seq, 2 * LANES_V7X), lambda p, b: (b, p, 0, 0)),
            pl.BlockSpec((1, 2, v_rows, seq), lambda p, b: (b, p, 0, 0)),
            pl.BlockSpec((1, n_blocks, LANES_V7X), lambda p, b: (b, 0, p)),
            pl.BlockSpec((1,) + tables.shape[1:], lambda p, b: (p, 0, 0),
                         pipeline_mode=pl.Buffered(1)),
        ],
        out_specs=pl.BlockSpec((1, seq, LANES_V7X), lambda p, b: (b, 0, p)),
        out_shape=jax.ShapeDtypeStruct((batch, seq, d), BF16),
        scratch_shapes=[
            pltpu.VMEM((2 * (seq // tq) - 1, tq, 2 * tq), F32),
        ],
        compiler_params=_params(2),
        name="moba_attention",
    )(q_t, kaug, v_t, kbar, tables)


def _tiles(seq):
    assert seq % (SUBLANES_V7X * MOBA_BLOCK) == 0, "qkv tile must hold 8 whole key blocks"
    return dict(tm_mix=min(seq, 512), tm_ffn=min(seq, 512), tm_qkv=SUBLANES_V7X * MOBA_BLOCK,
                tn_qkv=2 * LANES_V7X, fc_ffn=2 * LANES_V7X)


def kernel(x, ln_mix_g, ln_mix_b, ln_ffn_g, ln_ffn_b, a_w_in, a_ln_g, a_ln_b, a_w_s, a_b_s,
           a_w_out, b_w_qkv, b_w_o, rel_bias, f_w_up, f_conv_w, f_conv_b, f_w_down):
    batch, seq, d = x.shape
    depth = ln_mix_g.shape[0]
    alpha = (2 * depth) ** 0.25
    cfg = _tiles(seq)
    width = a_w_out.shape[1]
    n_groups = a_w_s.shape[1]

    def rows(p):
        return p[:, None, :]

    ln_mix_g, ln_mix_b, ln_ffn_g, ln_ffn_b, a_ln_g, a_ln_b, f_conv_b = (
        rows(p) for p in (ln_mix_g, ln_mix_b, ln_ffn_g, ln_ffn_b, a_ln_g, a_ln_b, f_conv_b))
    bs_tile = jnp.repeat(a_b_s.transpose(0, 2, 1), width // n_groups, axis=2)
    tables = _bias_tables(rel_bias)

    def layer_sources(i):
        mixer = (a_w_in, i // 2) if i % 2 == 0 else (b_w_qkv, i // 2)
        proj = (a_w_out, i // 2) if i % 2 == 0 else (b_w_o, i // 2)
        return [mixer, proj, (f_w_up, i), (f_w_down, i)]

    mixer_w = a_w_in[0].astype(BF16)
    channel_w = None
    h = x.reshape(batch * seq, d)
    for i in range(depth):
        j = i // 2
        first_casts = layer_sources(0)[1:] if i == 0 else []
        if i % 2 == 0:
            mix, casts = _gmlp_mixer(h, mixer_w, a_ln_g, a_ln_b, a_w_s, bs_tile, j, first_casts,
                                     tm=cfg["tm_mix"])
        else:
            qkv, casts = _qkv_proj(h, mixer_w, first_casts, batch=batch, seq=seq,
                                   tm=cfg["tm_qkv"], tn=cfg["tn_qkv"])
            mix = _moba_attention(*qkv, tables, batch=batch, seq=seq).reshape(batch * seq, d)
        w_proj, w_up, w_down = casts or channel_w
        next_sources = layer_sources(i + 1) if i + 1 < depth else []
        h, next_w = _channel_stage(mix, h, w_proj, ln_mix_g, ln_mix_b, w_up, f_conv_w, f_conv_b,
                                   w_down, ln_ffn_g, ln_ffn_b, i, next_sources, alpha=alpha,
                                   tm=cfg["tm_ffn"], fc=cfg["fc_ffn"], seq=seq)
        if next_w:
            mixer_w, *channel_w = next_w
    return h.reshape(batch, seq, d)
```

```python
import functools
import math

import jax
import jax.numpy as jnp
from jax import lax
from jax.experimental import pallas as pl
from jax.experimental.pallas import tpu as pltpu

F32 = jnp.float32
BF16 = jnp.bfloat16

GMLP_CHUNK = 128
N_HEADS = 16
MOBA_BLOCK = 256
MOBA_TOPK = 3
REL_BUCKETS = 32
REL_MAX_DIST = 128
LN_EPS = 1e-5

LANES_V7X = 128
SUBLANES_V7X = 8
BF16_SUBLANES_V7X = 16
VMEM_LIMIT_BYTES_V7X = 56 * 1024 * 1024

MASK_NEG = -1e30
LOG2_E = math.log2(math.e)


def _layer_norm(x, g, b):
    mu = jnp.mean(x, axis=-1, keepdims=True)
    xc = x - mu
    var = jnp.mean(xc * xc, axis=-1, keepdims=True)
    return xc * lax.rsqrt(var + LN_EPS) * g + b


def _layer_block(stacked, layer):
    tail = stacked.shape[1:]
    return pl.BlockSpec((None,) + tail, lambda *_: (layer,) + (0,) * len(tail),
                        pipeline_mode=pl.Buffered(1))


def _resident(array):
    return pl.BlockSpec(array.shape, lambda *_: (0,) * array.ndim, pipeline_mode=pl.Buffered(1))


def _params(n_axes):
    return pltpu.CompilerParams(
        dimension_semantics=("arbitrary",) * n_axes,
        vmem_limit_bytes=VMEM_LIMIT_BYTES_V7X,
    )


class _CastRide:
    def __init__(self, sources, n_chunks, chunk_of):
        self.args, self.in_specs, self.out_specs, self.out_shapes = [], [], [], []
        for stacked, layer in sources:
            _, r, c = stacked.shape
            share = next(s for s in range(1, n_chunks + 1)
                         if n_chunks % s == 0 and r % (n_chunks // s * BF16_SUBLANES_V7X) == 0)
            rows = r // (n_chunks // share)
            self.args.append(stacked)
            self.in_specs.append(pl.BlockSpec(
                (None, rows, c),
                lambda *g, layer=layer, share=share: (layer, chunk_of(*g) // share, 0)))
            self.out_specs.append(pl.BlockSpec(
                (rows, c), lambda *g, share=share: (chunk_of(*g) // share, 0)))
            self.out_shapes.append(jax.ShapeDtypeStruct((r, c), BF16))
        self.n = len(self.args)


def _cast_chunks(src_refs, dst_refs):
    for src, dst in zip(src_refs, dst_refs, strict=True):
        dst[...] = src[...].astype(dst.dtype)


def _gmlp_kernel(x_ref, w_in_ref, lng_ref, lnb_ref, ws_ref, bs_ref, *rest, n_groups, n_cast):
    cast_src, (o_ref, *cast_dst) = rest[:n_cast], rest[n_cast:]
    _cast_chunks(cast_src, cast_dst)
    x = x_ref[...]
    tm = x.shape[0]
    width = o_ref.shape[1]
    gdim = width // n_groups
    z = jnp.dot(x.astype(BF16), w_in_ref[...], preferred_element_type=F32)
    z = jax.nn.gelu(z)
    u = z[:, :width]
    v = _layer_norm(z[:, width:], lng_ref[...], lnb_ref[...]).astype(BF16)
    row = lax.broadcasted_iota(jnp.int32, (GMLP_CHUNK, GMLP_CHUNK), 0)
    col = lax.broadcasted_iota(jnp.int32, (GMLP_CHUNK, GMLP_CHUNK), 1)
    causal = col <= row
    w_masked = [jnp.where(causal, ws_ref[g], 0.0).astype(BF16) for g in range(n_groups)]
    chunks = []
    for c in range(tm // GMLP_CHUNK):
        rows = slice(c * GMLP_CHUNK, (c + 1) * GMLP_CHUNK)
        groups = [
            jnp.dot(w_masked[g], v[rows, g * gdim:(g + 1) * gdim], preferred_element_type=F32)
            for g in range(n_groups)
        ]
        chunks.append(jnp.concatenate(groups, axis=1) + bs_ref[...])
    sv = jnp.concatenate(chunks, axis=0)
    o_ref[...] = (u * sv).astype(o_ref.dtype)


def _gmlp_mixer(x, w_in, ln_g, ln_b, w_s, bs_tile, layer, cast_sources, *, tm):
    t, d = x.shape
    width = w_in.shape[1] // 2
    n_steps = t // tm
    ride = _CastRide(cast_sources, n_steps, lambda i: i)
    mix, *casts = pl.pallas_call(
        functools.partial(_gmlp_kernel, n_groups=w_s.shape[1], n_cast=ride.n),
        grid=(n_steps,),
        in_specs=[pl.BlockSpec((tm, d), lambda i: (i, 0)), _resident(w_in)]
        + [_layer_block(a, layer) for a in (ln_g, ln_b, w_s, bs_tile)] + ride.in_specs,
        out_specs=[pl.BlockSpec((tm, width), lambda i: (i, 0))] + ride.out_specs,
        out_shape=[jax.ShapeDtypeStruct((t, width), BF16)] + ride.out_shapes,
        compiler_params=_params(1),
        name="gmlp_mixer",
    )(x, w_in, ln_g, ln_b, w_s, bs_tile, *ride.args)
    return mix, casts


def _causal_conv(h, prev, cw, cb):
    head = jnp.concatenate([prev, h[:SUBLANES_V7X]], axis=0)

    def shifted(rows):
        top = pltpu.roll(head, rows, 0)[SUBLANES_V7X:]
        return jnp.concatenate([top, pltpu.roll(h, rows, 0)[SUBLANES_V7X:]], axis=0).astype(BF16)

    cw = cw.astype(BF16)
    return (cb.astype(BF16) + cw[0:1] * shifted(2) + cw[1:2] * shifted(1)
            + cw[2:3] * h.astype(BF16))


def _channel_kernel(mix0_ref, x0_ref, mix_ref, x_ref, w_proj_ref, g_mix_ref, b_mix_ref,
                    w_up_ref, cw_ref, cb_ref, w_down_ref, g_ffn_ref, b_ffn_ref, *rest,
                    alpha, fc, tiles_per_seq, n_tiles, n_cast):
    cast_src, (o_ref, *rest) = rest[:n_cast], rest[n_cast:]
    cast_dst, (xmid_ref, xnext_ref, ysum_ref, act_ref, carry_ref) = rest[:n_cast], rest[n_cast:]
    _cast_chunks(cast_src, cast_dst)
    step = pl.program_id(0)
    tm = x_ref.shape[0]
    ffn = w_down_ref.shape[0]

    def mixer_sublayer(mix, x):
        y = jnp.dot(mix, w_proj_ref[...], preferred_element_type=F32)
        return _layer_norm(alpha * x + y, g_mix_ref[...], b_mix_ref[...])

    def final_norm():
        return _layer_norm(ysum_ref[...], g_ffn_ref[...], b_ffn_ref[...])

    @pl.when(step == 0)
    def _():
        xmid_ref[...] = mixer_sublayer(mix0_ref[...], x0_ref[...])
        ysum_ref[...] = jnp.zeros_like(ysum_ref)

    @pl.when(step < n_tiles)
    def _():
        @pl.when(step % tiles_per_seq == 0)
        def _():
            carry_ref[...] = jnp.zeros_like(carry_ref)

        x = xmid_ref[...]
        xb = x.astype(BF16)

        def up_proj(j):
            return tuple(
                jnp.dot(xb, w_up_ref[:, start:start + fc], preferred_element_type=F32)
                for start in (j * fc, ffn + j * fc))

        def conv_cols(h, start):
            cols = slice(start, start + fc)
            prev = carry_ref[:, cols]
            carry_ref[:, cols] = h[tm - SUBLANES_V7X:]
            return _causal_conv(h, prev, cw_ref[:, cols], cb_ref[:, cols])

        n_chunks = ffn // fc
        h_pair = up_proj(0)
        for j in range(n_chunks):
            h_gate, h_val = h_pair
            if j + 1 < n_chunks:
                h_pair = up_proj(j + 1)
            gate = conv_cols(h_gate, j * fc)
            val = conv_cols(h_val, ffn + j * fc)
            act_ref[:, j * fc:(j + 1) * fc] = jax.nn.gelu(gate) * val

        o_ref[...] = final_norm()
        xnext_ref[...] = mixer_sublayer(mix_ref[...], x_ref[...])
        y = jnp.dot(act_ref[...], w_down_ref[...], preferred_element_type=F32)
        ysum_ref[...] = alpha * x + y
        xmid_ref[...] = xnext_ref[...]

    @pl.when(step == n_tiles)
    def _():
        o_ref[...] = final_norm()


def _channel_stage(mix, x, w_proj, ln_mix_g, ln_mix_b, w_up, conv_w, conv_b, w_down,
                   ln_ffn_g, ln_ffn_b, layer, cast_sources, *, alpha, tm, fc, seq):
    t, d = x.shape
    ffn = w_down.shape[0]
    n_tiles = t // tm
    width = mix.shape[1]
    ride = _CastRide(cast_sources, n_tiles, lambda i: jnp.minimum(i, n_tiles - 1))

    def first_tile(cols):
        return pl.BlockSpec((tm, cols), lambda i: (0, 0), pipeline_mode=pl.Buffered(1))

    def next_tile(cols):
        return pl.BlockSpec((tm, cols), lambda i: (jnp.minimum(i + 1, n_tiles - 1), 0))

    def layer_rows(p):
        return _layer_block(p, layer)

    out, *casts = pl.pallas_call(
        functools.partial(_channel_kernel, alpha=alpha, fc=fc, tiles_per_seq=seq // tm,
                          n_tiles=n_tiles, n_cast=ride.n),
        grid=(n_tiles + 1,),
        in_specs=[first_tile(width), first_tile(d), next_tile(width), next_tile(d),
                  _resident(w_proj), layer_rows(ln_mix_g), layer_rows(ln_mix_b),
                  _resident(w_up), layer_rows(conv_w), layer_rows(conv_b), _resident(w_down),
                  layer_rows(ln_ffn_g), layer_rows(ln_ffn_b)] + ride.in_specs,
        out_specs=[pl.BlockSpec((tm, d), lambda i: (jnp.maximum(i - 1, 0), 0))] + ride.out_specs,
        out_shape=[jax.ShapeDtypeStruct((t, d), F32)] + ride.out_shapes,
        scratch_shapes=[
            pltpu.VMEM((tm, d), F32),
            pltpu.VMEM((tm, d), F32),
            pltpu.VMEM((tm, d), F32),
            pltpu.VMEM((tm, ffn), BF16),
            pltpu.VMEM((SUBLANES_V7X, 2 * ffn), F32),
        ],
        compiler_params=_params(1),
        name="channel_stage",
    )(mix, x, mix, x, w_proj, ln_mix_g, ln_mix_b, w_up, conv_w, conv_b, w_down,
      ln_ffn_g, ln_ffn_b, *ride.args)
    return out, casts


def _qkv_kernel(x_ref, wq_ref, wk_ref, wv_ref, *rest, head_dim, tiles_per_seq, n_cast):
    cast_src, (q_ref, k_ref, v_ref, kbar_ref, *cast_dst) = rest[:n_cast], rest[n_cast:]
    _cast_chunks(cast_src, cast_dst)
    q_scale = head_dim ** -0.5 * LOG2_E
    tm = x_ref.shape[0]
    tn = wq_ref.shape[1]
    n_pairs = tn // LANES_V7X
    n_heads = tn // head_dim
    tile_pos = (pl.program_id(0) % tiles_per_seq) * tm
    sub = 2 * MOBA_BLOCK
    lane = lax.broadcasted_iota(jnp.int32, (MOBA_BLOCK, LANES_V7X), 1)

    def project(r):
        xb = x_ref[r * sub:(r + 1) * sub, :].astype(BF16)
        return tuple(jnp.dot(xb, w_ref[...], preferred_element_type=F32)
                     for w_ref in (wq_ref, wk_ref, wv_ref))

    kbar = []
    n_sub = tm // sub
    projected = project(0)
    for r in range(n_sub):
        rows = slice(r * sub, (r + 1) * sub)
        q, k, v = projected
        if r + 1 < n_sub:
            projected = project(r + 1)
        q = q * q_scale
        first_block = (tile_pos + r * sub) // MOBA_BLOCK
        block_onehot = jnp.concatenate(
            [jnp.where(lane == first_block + b, 1.0, 0.0).astype(BF16)
             for b in range(sub // MOBA_BLOCK)], axis=0)
        q_t = q.T.astype(BF16)
        v_t = v.T.astype(BF16)
        for p in range(n_pairs):
            cols = slice(p * LANES_V7X, (p + 1) * LANES_V7X)
            q_ref[0, p, :, rows] = q_t[cols]
            k_ref[0, p, rows, :LANES_V7X] = k[:, cols].astype(BF16)
            k_ref[0, p, rows, LANES_V7X:] = block_onehot
        for h in range(n_heads):
            v_ref[0, h, :head_dim, rows] = v_t[h * head_dim:(h + 1) * head_dim]
            v_ref[0, h, head_dim:, rows] = jnp.ones((v_ref.shape[2] - head_dim, sub), BF16)
        kbar.append(jnp.mean(k.reshape(sub // MOBA_BLOCK, MOBA_BLOCK, tn), axis=1))
    kbar_ref[0] = jnp.concatenate(kbar, axis=0)


def _qkv_proj(x, w_qkv, cast_sources, *, batch, seq, tm, tn):
    t, d = x.shape
    n_pairs = d // LANES_V7X
    pairs_per_step = tn // LANES_V7X
    tiles_per_seq = seq // tm
    n_col = d // tn
    head_dim = d // N_HEADS
    heads_per_step = tn // head_dim
    v_rows = head_dim + BF16_SUBLANES_V7X

    def row_major_map(i, j):
        return (i // tiles_per_seq, j, i % tiles_per_seq, 0)

    def feature_major_map(i, j):
        return (i // tiles_per_seq, j, 0, i % tiles_per_seq)

    def w_cols(which):
        return pl.BlockSpec((d, tn), lambda i, j: (0, which * n_col + j))

    n_rows = t // tm
    ride = _CastRide(cast_sources, n_rows * n_col, lambda i, j: i * n_col + j)
    q_t, kaug, v_t, kbar, *casts = pl.pallas_call(
        functools.partial(_qkv_kernel, head_dim=head_dim, tiles_per_seq=tiles_per_seq,
                          n_cast=ride.n),
        grid=(n_rows, n_col),
        in_specs=[pl.BlockSpec((tm, d), lambda i, j: (i, 0)), w_cols(0), w_cols(1), w_cols(2)]
        + ride.in_specs,
        out_specs=[
            pl.BlockSpec((1, pairs_per_step, LANES_V7X, tm), feature_major_map),
            pl.BlockSpec((1, pairs_per_step, tm, 2 * LANES_V7X), row_major_map),
            pl.BlockSpec((1, heads_per_step, v_rows, tm), feature_major_map),
            pl.BlockSpec((1, tm // MOBA_BLOCK, tn),
                         lambda i, j: (i // tiles_per_seq, i % tiles_per_seq, j)),
        ] + ride.out_specs,
        out_shape=[
            jax.ShapeDtypeStruct((batch, n_pairs, LANES_V7X, seq), BF16),
            jax.ShapeDtypeStruct((batch, n_pairs, seq, 2 * LANES_V7X), BF16),
            jax.ShapeDtypeStruct((batch, N_HEADS, v_rows, seq), BF16),
            jax.ShapeDtypeStruct((batch, seq // MOBA_BLOCK, d), F32),
        ] + ride.out_shapes,
        compiler_params=_params(2),
        name="moba_qkv_proj",
    )(x, w_qkv, w_qkv, w_qkv, *ride.args)
    return (q_t, kaug, v_t, kbar), casts


def _moba_attn_kernel(q_ref, k_ref, v_ref, kbar_ref, tab_ref, o_ref, s_ref, *, head_dim):
    blk_len = MOBA_BLOCK
    tq = pair_len = 2 * blk_len
    n_tiles = q_ref.shape[3] // tq

    def slots(tile):
        return 0 if tile % 2 == 0 else n_tiles - 1

    def head_cols(h):
        return slice(h * tq, (h + 1) * tq)

    def score_unit(tile, pair, h, q_aug, cmax):
        s = jnp.dot(k_ref[0, 0, pair * pair_len:(pair + 1) * pair_len, :], q_aug[:, head_cols(h)],
                    preferred_element_type=F32)
        if pair == tile:
            s = s + tab_ref[0, blk_len:, head_cols(h)]
        elif pair == tile - 1:
            s = jnp.concatenate(
                [s[:blk_len], s[blk_len:] + tab_ref[0, :blk_len, head_cols(h)]], axis=0)
        s_ref[slots(tile) + pair, :, head_cols(h)] = s
        folded = jnp.max(s.reshape(pair_len // SUBLANES_V7X, SUBLANES_V7X, tq), axis=0)
        return folded if cmax is None else jnp.maximum(cmax, folded)

    def value_unit(tile, pair, h, col_m, acc):
        s = s_ref[slots(tile) + pair, :, head_cols(h)]
        p = jnp.exp2((s - col_m).astype(BF16))
        pv = jnp.dot(v_ref[0, h, :, pair * pair_len:(pair + 1) * pair_len], p,
                     preferred_element_type=F32)
        return pv if acc is None else acc + pv

    def finish(tile, acc):
        heads = [a[:head_dim] / a[head_dim:head_dim + 1] for a in acc]
        o_ref[0, tile * tq:(tile + 1) * tq, :] = (
            jnp.concatenate(heads, axis=0).T.astype(o_ref.dtype))

    col_m = [None, None]
    for step in range(n_tiles + 1):
        scoring, valuing = step, step - 1
        n_score = scoring + 1 if scoring < n_tiles else 0
        n_value = valuing + 1
        q_aug = _tile_queries(scoring, q_ref, kbar_ref, head_dim) if n_score else None
        cmax, acc = [None, None], [None, None]
        for pair in range(max(n_score, n_value)):
            for h in range(2):
                if pair < n_score:
                    cmax[h] = score_unit(scoring, pair, h, q_aug, cmax[h])
                if pair < n_value:
                    acc[h] = value_unit(valuing, pair, h, col_m[h], acc[h])
        if n_value:
            finish(valuing, acc)
        if n_score:
            col_m = [jnp.max(c, axis=0, keepdims=True) for c in cmax]


def _tile_queries(tile, q_ref, kbar_ref, head_dim):
    blk_len = MOBA_BLOCK
    tq = 2 * blk_len
    q_t = q_ref[0, 0, :, tile * tq:(tile + 1) * tq]
    n_blocks = kbar_ref.shape[1]
    feat_row = lax.broadcasted_iota(jnp.int32, q_t.shape, 0)
    kbar = kbar_ref[0].astype(BF16)
    blk = lax.broadcasted_iota(jnp.int32, (n_blocks, tq), 0)
    query = lax.broadcasted_iota(jnp.int32, (n_blocks, tq), 1)
    cur = 2 * tile + jnp.where(query >= blk_len, 1, 0)

    q_aug = []
    for h in range(2):
        in_head = (feat_row >= head_dim) if h else (feat_row < head_dim)
        qm = jnp.where(in_head, q_t, jnp.zeros_like(q_t))
        gate = jnp.dot(kbar, qm, preferred_element_type=F32)
        gate = jnp.where(blk < cur, gate, -jnp.inf)
        rank = jnp.zeros(gate.shape, F32)
        for m in range(min(n_blocks, 2 * tile + 1)):
            gm = gate[m:m + 1, :]
            tie = jnp.where(blk > m, 1.0, 0.0)
            rank = rank + jnp.where(gm > gate, 1.0, jnp.where(gm == gate, tie, 0.0))
        feat = jnp.where(blk < cur,
                         jnp.where(rank < MOBA_TOPK, 0.0, MASK_NEG),
                         jnp.where(blk == cur, 0.0, MASK_NEG))
        feat = jnp.concatenate(
            [feat, jnp.zeros((LANES_V7X - n_blocks, tq), F32)], axis=0)
        q_aug.append(jnp.concatenate([qm, feat.astype(BF16)], axis=0))
    return jnp.concatenate(q_aug, axis=1)


def _rel_bucket(dist):
    n = jnp.maximum(dist, 0)
    max_exact = REL_BUCKETS // 2
    nf = jnp.maximum(n, 1).astype(F32)
    large = max_exact + (jnp.log(nf / max_exact) / math.log(REL_MAX_DIST / max_exact)
                         * (REL_BUCKETS - max_exact)).astype(jnp.int32)
    large = jnp.minimum(large, REL_BUCKETS - 1)
    return jnp.where(n < max_exact, n, large)


def _bias_tables(rel_bias):
    blk = MOBA_BLOCK
    n_keys, n_q = 3 * blk, 2 * blk
    n_heads = rel_bias.shape[1]
    bias_h = rel_bias.T
    d_min = blk - (n_keys - 1)
    dist = jnp.arange(d_min, blk + n_q)
    by_dist = (bias_h[:, _rel_bucket(dist)] - bias_h[:, REL_BUCKETS - 1:]) * LOG2_E
    by_dist = jnp.where(dist[None] < 0, MASK_NEG, by_dist)
    period = by_dist.shape[1] + 1
    assert period % LANES_V7X == 0 and (period - n_q) % LANES_V7X == 0
    rows = jnp.pad(by_dist, ((0, 0), (0, 1)))[:, None, :]
    shift = (period - n_q) - (blk - d_min)

    def expand(rows_ref, out_ref):
        for h in range(2):
            wide = jnp.broadcast_to(rows_ref[h], (n_keys, period))
            rolled = pltpu.roll(wide, shift, 1, stride=1, stride_axis=0)
            out_ref[0, :, h * n_q:(h + 1) * n_q] = rolled[:, period - n_q:]

    return pl.pallas_call(
        expand,
        grid=(n_heads // 2,),
        in_specs=[pl.BlockSpec((2, 1, period), lambda p: (p, 0, 0))],
        out_specs=pl.BlockSpec((1, n_keys, 2 * n_q), lambda p: (p, 0, 0)),
        out_shape=jax.ShapeDtypeStruct((n_heads // 2, n_keys, 2 * n_q), F32),
        compiler_params=_params(1),
        name="rel_bias_tables",
    )(rows)


def _moba_attention(q_t, kaug, v_t, kbar, tables, *, batch, seq):
    n_pairs = q_t.shape[1]
    d = n_pairs * LANES_V7X
    tq = 2 * MOBA_BLOCK
    n_blocks = seq // MOBA_BLOCK
    v_rows = v_t.shape[2]
    return pl.pallas_call(
        functools.partial(_moba_attn_kernel, head_dim=d // N_HEADS),
        grid=(n_pairs, batch),
        in_specs=[
            pl.BlockSpec((1, 1, LANES_V7X, seq), lambda p, b: (b, p, 0, 0)),
            pl.BlockSpec((1, 1, seq, 2 * LANES_V7X), lambda p, b: (b, p, 0, 0)),
            pl.BlockSpec((1, 2, v_rows, seq), lambda p, b: (b, p, 0, 0)),
            pl.BlockSpec((1, n_blocks, LANES_V7X), lambda p, b: (b, 0, p)),
            pl.BlockSpec((1,) + tables.shape[1:], lambda p, b: (p, 0, 0),
                         pipeline_mode=pl.Buffered(1)),
        ],
        out_specs=pl.BlockSpec((1, seq, LANES_V7X), lambda p, b: (b, 0, p)),
        out_shape=jax.ShapeDtypeStruct((batch, seq, d), BF16),
        scratch_shapes=[
            pltpu.VMEM((2 * (seq // tq) - 1, tq, 2 * tq), F32),
        ],
        compiler_params=_params(2),
        name="moba_attention",
    )(q_t, kaug, v_t, kbar, tables)


def _tiles(seq):
    assert seq % (SUBLANES_V7X * MOBA_BLOCK) == 0, "qkv tile must hold 8 whole key blocks"
    return dict(tm_mix=min(seq, 512), tm_ffn=min(seq, 512), tm_qkv=SUBLANES_V7X * MOBA_BLOCK,
                tn_qkv=2 * LANES_V7X, fc_ffn=2 * LANES_V7X)


def kernel(x, ln_mix_g, ln_mix_b, ln_ffn_g, ln_ffn_b, a_w_in, a_ln_g, a_ln_b, a_w_s, a_b_s,
           a_w_out, b_w_qkv, b_w_o, rel_bias, f_w_up, f_conv_w, f_conv_b, f_w_down):
    batch, seq, d = x.shape
    depth = ln_mix_g.shape[0]
    alpha = (2 * depth) ** 0.25
    cfg = _tiles(seq)
    width = a_w_out.shape[1]
    n_groups = a_w_s.shape[1]

    def rows(p):
        return p[:, None, :]

    ln_mix_g, ln_mix_b, ln_ffn_g, ln_ffn_b, a_ln_g, a_ln_b, f_conv_b = (
        rows(p) for p in (ln_mix_g, ln_mix_b, ln_ffn_g, ln_ffn_b, a_ln_g, a_ln_b, f_conv_b))
    bs_tile = jnp.repeat(a_b_s.transpose(0, 2, 1), width // n_groups, axis=2)
    tables = _bias_tables(rel_bias)

    def layer_sources(i):
        mixer = (a_w_in, i // 2) if i % 2 == 0 else (b_w_qkv, i // 2)
        proj = (a_w_out, i // 2) if i % 2 == 0 else (b_w_o, i // 2)
        return [mixer, proj, (f_w_up, i), (f_w_down, i)]

    mixer_w = a_w_in[0].astype(BF16)
    channel_w = None
    h = x.reshape(batch * seq, d)
    for i in range(depth):
        j = i // 2
        first_casts = layer_sources(0)[1:] if i == 0 else []
        if i % 2 == 0:
            mix, casts = _gmlp_mixer(h, mixer_w, a_ln_g, a_ln_b, a_w_s, bs_tile, j, first_casts,
                                     tm=cfg["tm_mix"])
        else:
            qkv, casts = _qkv_proj(h, mixer_w, first_casts, batch=batch, seq=seq,
                                   tm=cfg["tm_qkv"], tn=cfg["tn_qkv"])
            mix = _moba_attention(*qkv, tables, batch=batch, seq=seq).reshape(batch * seq, d)
        w_proj, w_up, w_down = casts or channel_w
        next_sources = layer_sources(i + 1) if i + 1 < depth else []
        h, next_w = _channel_stage(mix, h, w_proj, ln_mix_g, ln_mix_b, w_up, f_conv_w, f_conv_b,
                                   w_down, ln_ffn_g, ln_ffn_b, i, next_sources, alpha=alpha,
                                   tm=cfg["tm_ffn"], fc=cfg["fc_ffn"], seq=seq)
        if next_w:
            mixer_w, *channel_w = next_w
    return h.reshape(batch, seq, d)
```

```python
import functools
import math

import jax
import jax.numpy as jnp
from jax import lax
from jax.experimental import pallas as pl
from jax.experimental.pallas import tpu as pltpu

F32 = jnp.float32
BF16 = jnp.bfloat16

GMLP_CHUNK = 128
N_HEADS = 16
MOBA_BLOCK = 256
MOBA_TOPK = 3
REL_BUCKETS = 32
REL_MAX_DIST = 128
LN_EPS = 1e-5

LANES_V7X = 128
SUBLANES_V7X = 8
BF16_SUBLANES_V7X = 16
VMEM_LIMIT_BYTES_V7X = 56 * 1024 * 1024

MASK_NEG = -1e30
LOG2_E = math.log2(math.e)


def _layer_norm(x, g, b):
    mu = jnp.mean(x, axis=-1, keepdims=True)
    xc = x - mu
    var = jnp.mean(xc * xc, axis=-1, keepdims=True)
    return xc * lax.rsqrt(var + LN_EPS) * g + b


def _layer_block(stacked, layer):
    tail = stacked.shape[1:]
    return pl.BlockSpec((None,) + tail, lambda *_: (layer,) + (0,) * len(tail),
                        pipeline_mode=pl.Buffered(1))


def _resident(array):
    return pl.BlockSpec(array.shape, lambda *_: (0,) * array.ndim, pipeline_mode=pl.Buffered(1))


def _params(n_axes):
    return pltpu.CompilerParams(
        dimension_semantics=("arbitrary",) * n_axes,
        vmem_limit_bytes=VMEM_LIMIT_BYTES_V7X,
    )


class _CastRide:
    def __init__(self, sources, n_chunks, chunk_of):
        self.args, self.in_specs, self.out_specs, self.out_shapes = [], [], [], []
        for stacked, layer in sources:
            _, r, c = stacked.shape
            share = next(s for s in range(1, n_chunks + 1)
                         if n_chunks % s == 0 and r % (n_chunks // s * BF16_SUBLANES_V7X) == 0)
            rows = r // (n_chunks // share)
            self.args.append(stacked)
            self.in_specs.append(pl.BlockSpec(
                (None, rows, c),
                lambda *g, layer=layer, share=share: (layer, chunk_of(*g) // share, 0)))
            self.out_specs.append(pl.BlockSpec(
                (rows, c), lambda *g, share=share: (chunk_of(*g) // share, 0)))
            self.out_shapes.append(jax.ShapeDtypeStruct((r, c), BF16))
        self.n = len(self.args)


def _cast_chunks(src_refs, dst_refs):
    for src, dst in zip(src_refs, dst_refs, strict=True):
        dst[...] = src[...].astype(dst.dtype)


def _gmlp_kernel(x_ref, w_in_ref, lng_ref, lnb_ref, ws_ref, bs_ref, *rest, n_groups, n_cast):
    cast_src, (o_ref, *cast_dst) = rest[:n_cast], rest[n_cast:]
    _cast_chunks(cast_src, cast_dst)
    x = x_ref[...]
    tm = x.shape[0]
    width = o_ref.shape[1]
    gdim = width // n_groups
    z = jnp.dot(x.astype(BF16), w_in_ref[...], preferred_element_type=F32)
    z = jax.nn.gelu(z)
    u = z[:, :width]
    v = _layer_norm(z[:, width:], lng_ref[...], lnb_ref[...]).astype(BF16)
    row = lax.broadcasted_iota(jnp.int32, (GMLP_CHUNK, GMLP_CHUNK), 0)
    col = lax.broadcasted_iota(jnp.int32, (GMLP_CHUNK, GMLP_CHUNK), 1)
    causal = col <= row
    w_masked = [jnp.where(causal, ws_ref[g], 0.0).astype(BF16) for g in range(n_groups)]
    chunks = []
    for c in range(tm // GMLP_CHUNK):
        rows = slice(c * GMLP_CHUNK, (c + 1) * GMLP_CHUNK)
        groups = [
            jnp.dot(w_masked[g], v[rows, g * gdim:(g + 1) * gdim], preferred_element_type=F32)
            for g in range(n_groups)
        ]
        chunks.append(jnp.concatenate(groups, axis=1) + bs_ref[...])
    sv = jnp.concatenate(chunks, axis=0)
    o_ref[...] = (u * sv).astype(o_ref.dtype)


def _gmlp_mixer(x, w_in, ln_g, ln_b, w_s, bs_tile, layer, cast_sources, *, tm):
    t, d = x.shape
    width = w_in.shape[1] // 2
    n_steps = t // tm
    ride = _CastRide(cast_sources, n_steps, lambda i: i)
    mix, *casts = pl.pallas_call(
        functools.partial(_gmlp_kernel, n_groups=w_s.shape[1], n_cast=ride.n),
        grid=(n_steps,),
        in_specs=[pl.BlockSpec((tm, d), lambda i: (i, 0)), _resident(w_in)]
        + [_layer_block(a, layer) for a in (ln_g, ln_b, w_s, bs_tile)] + ride.in_specs,
        out_specs=[pl.BlockSpec((tm, width), lambda i: (i, 0))] + ride.out_specs,
        out_shape=[jax.ShapeDtypeStruct((t, width), BF16)] + ride.out_shapes,
        compiler_params=_params(1),
        name="gmlp_mixer",
    )(x, w_in, ln_g, ln_b, w_s, bs_tile, *ride.args)
    return mix, casts


def _causal_conv(h, prev, cw, cb):
    head = jnp.concatenate([prev, h[:SUBLANES_V7X]], axis=0)

    def shifted(rows):
        top = pltpu.roll(head, rows, 0)[SUBLANES_V7X:]
        return jnp.concatenate([top, pltpu.roll(h, rows, 0)[SUBLANES_V7X:]], axis=0).astype(BF16)

    cw = cw.astype(BF16)
    return (cb.astype(BF16) + cw[0:1] * shifted(2) + cw[1:2] * shifted(1)
            + cw[2:3] * h.astype(BF16))


def _channel_kernel(mix0_ref, x0_ref, mix_ref, x_ref, w_proj_ref, g_mix_ref, b_mix_ref,
                    w_up_ref, cw_ref, cb_ref, w_down_ref, g_ffn_ref, b_ffn_ref, *rest,
                    alpha, fc, tiles_per_seq, n_tiles, n_cast):
    cast_src, (o_ref, *rest) = rest[:n_cast], rest[n_cast:]
    cast_dst, (xmid_ref, xnext_ref, ysum_ref, act_ref, carry_ref) = rest[:n_cast], rest[n_cast:]
    _cast_chunks(cast_src, cast_dst)
    step = pl.program_id(0)
    tm = x_ref.shape[0]
    ffn = w_down_ref.shape[0]

    def mixer_sublayer(mix, x):
        y = jnp.dot(mix, w_proj_ref[...], preferred_element_type=F32)
        return _layer_norm(alpha * x + y, g_mix_ref[...], b_mix_ref[...])

    def final_norm():
        return _layer_norm(ysum_ref[...], g_ffn_ref[...], b_ffn_ref[...])

    @pl.when(step == 0)
    def _():
        xmid_ref[...] = mixer_sublayer(mix0_ref[...], x0_ref[...])
        ysum_ref[...] = jnp.zeros_like(ysum_ref)

    @pl.when(step < n_tiles)
    def _():
        @pl.when(step % tiles_per_seq == 0)
        def _():
            carry_ref[...] = jnp.zeros_like(carry_ref)

        x = xmid_ref[...]
        xb = x.astype(BF16)

        def up_proj(j):
            return tuple(
                jnp.dot(xb, w_up_ref[:, start:start + fc], preferred_element_type=F32)
                for start in (j * fc, ffn + j * fc))

        def conv_cols(h, start):
            cols = slice(start, start + fc)
            prev = carry_ref[:, cols]
            carry_ref[:, cols] = h[tm - SUBLANES_V7X:]
            return _causal_conv(h, prev, cw_ref[:, cols], cb_ref[:, cols])

        n_chunks = ffn // fc
        h_pair = up_proj(0)
        for j in range(n_chunks):
            h_gate, h_val = h_pair
            if j + 1 < n_chunks:
                h_pair = up_proj(j + 1)
            gate = conv_cols(h_gate, j * fc)
            val = conv_cols(h_val, ffn + j * fc)
            act_ref[:, j * fc:(j + 1) * fc] = jax.nn.gelu(gate) * val

        o_ref[...] = final_norm()
        xnext_ref[...] = mixer_sublayer(mix_ref[...], x_ref[...])
        y = jnp.dot(act_ref[...], w_down_ref[...], preferred_element_type=F32)
        ysum_ref[...] = alpha * x + y
        xmid_ref[...] = xnext_ref[...]

    @pl.when(step == n_tiles)
    def _():
        o_ref[...] = final_norm()


def _channel_stage(mix, x, w_proj, ln_mix_g, ln_mix_b, w_up, conv_w, conv_b, w_down,
                   ln_ffn_g, ln_ffn_b, layer, cast_sources, *, alpha, tm, fc, seq):
    t, d = x.shape
    ffn = w_down.shape[0]
    n_tiles = t // tm
    width = mix.shape[1]
    ride = _CastRide(cast_sources, n_tiles, lambda i: jnp.minimum(i, n_tiles - 1))

    def first_tile(cols):
        return pl.BlockSpec((tm, cols), lambda i: (0, 0), pipeline_mode=pl.Buffered(1))

    def next_tile(cols):
        return pl.BlockSpec((tm, cols), lambda i: (jnp.minimum(i + 1, n_tiles - 1), 0))

    def layer_rows(p):
        return _layer_block(p, layer)

    out, *casts = pl.pallas_call(
        functools.partial(_channel_kernel, alpha=alpha, fc=fc, tiles_per_seq=seq // tm,
                          n_tiles=n_tiles, n_cast=ride.n),
        grid=(n_tiles + 1,),
        in_specs=[first_tile(width), first_tile(d), next_tile(width), next_tile(d),
                  _resident(w_proj), layer_rows(ln_mix_g), layer_rows(ln_mix_b),
                  _resident(w_up), layer_rows(conv_w), layer_rows(conv_b), _resident(w_down),
                  layer_rows(ln_ffn_g), layer_rows(ln_ffn_b)] + ride.in_specs,
        out_specs=[pl.BlockSpec((tm, d), lambda i: (jnp.maximum(i - 1, 0), 0))] + ride.out_specs,
        out_shape=[jax.ShapeDtypeStruct((t, d), F32)] + ride.out_shapes,
        scratch_shapes=[
            pltpu.VMEM((tm, d), F32),
            pltpu.VMEM((tm, d), F32),
            pltpu.VMEM((tm, d), F32),
            pltpu.VMEM((tm, ffn), BF16),
            pltpu.VMEM((SUBLANES_V7X, 2 * ffn), F32),
        ],
        compiler_params=_params(1),
        name="channel_stage",
    )(mix, x, mix, x, w_proj, ln_mix_g, ln_mix_b, w_up, conv_w, conv_b, w_down,
      ln_ffn_g, ln_ffn_b, *ride.args)
    return out, casts


def _qkv_kernel(x_ref, wq_ref, wk_ref, wv_ref, *rest, head_dim, tiles_per_seq, n_cast):
    cast_src, (q_ref, k_ref, v_ref, kbar_ref, *cast_dst) = rest[:n_cast], rest[n_cast:]
    _cast_chunks(cast_src, cast_dst)
    q_scale = head_dim ** -0.5 * LOG2_E
    tm = x_ref.shape[0]
    tn = wq_ref.shape[1]
    n_pairs = tn // LANES_V7X
    n_heads = tn // head_dim
    tile_pos = (pl.program_id(0) % tiles_per_seq) * tm
    sub = 2 * MOBA_BLOCK
    lane = lax.broadcasted_iota(jnp.int32, (MOBA_BLOCK, LANES_V7X), 1)

    def project(r):
        xb = x_ref[r * sub:(r + 1) * sub, :].astype(BF16)
        return tuple(jnp.dot(xb, w_ref[...], preferred_element_type=F32)
                     for w_ref in (wq_ref, wk_ref, wv_ref))

    kbar = []
    n_sub = tm // sub
    projected = project(0)
    for r in range(n_sub):
        rows = slice(r * sub, (r + 1) * sub)
        q, k, v = projected
        if r + 1 < n_sub:
            projected = project(r + 1)
        q = q * q_scale
        first_block = (tile_pos + r * sub) // MOBA_BLOCK
        block_onehot = jnp.concatenate(
            [jnp.where(lane == first_block + b, 1.0, 0.0).astype(BF16)
             for b in range(sub // MOBA_BLOCK)], axis=0)
        q_t = q.T.astype(BF16)
        v_t = v.T.astype(BF16)
        for p in range(n_pairs):
            cols = slice(p * LANES_V7X, (p + 1) * LANES_V7X)
            q_ref[0, p, :, rows] = q_t[cols]
            k_ref[0, p, rows, :LANES_V7X] = k[:, cols].astype(BF16)
            k_ref[0, p, rows, LANES_V7X:] = block_onehot
        for h in range(n_heads):
            v_ref[0, h, :head_dim, rows] = v_t[h * head_dim:(h + 1) * head_dim]
            v_ref[0, h, head_dim:, rows] = jnp.ones((v_ref.shape[2] - head_dim, sub), BF16)
        kbar.append(jnp.mean(k.reshape(sub // MOBA_BLOCK, MOBA_BLOCK, tn), axis=1))
    kbar_ref[0] = jnp.concatenate(kbar, axis=0)


def _qkv_proj(x, w_qkv, cast_sources, *, batch, seq, tm, tn):
    t, d = x.shape
    n_pairs = d // LANES_V7X
    pairs_per_step = tn // LANES_V7X
    tiles_per_seq = seq // tm
    n_col = d // tn
    head_dim = d // N_HEADS
    heads_per_step = tn // head_dim
    v_rows = head_dim + BF16_SUBLANES_V7X

    def row_major_map(i, j):
        return (i // tiles_per_seq, j, i % tiles_per_seq, 0)

    def feature_major_map(i, j):
        return (i // tiles_per_seq, j, 0, i % tiles_per_seq)

    def w_cols(which):
        return pl.BlockSpec((d, tn), lambda i, j: (0, which * n_col + j))

    n_rows = t // tm
    ride = _CastRide(cast_sources, n_rows * n_col, lambda i, j: i * n_col + j)
    q_t, kaug, v_t, kbar, *casts = pl.pallas_call(
        functools.partial(_qkv_kernel, head_dim=head_dim, tiles_per_seq=tiles_per_seq,
                          n_cast=ride.n),
        grid=(n_rows, n_col),
        in_specs=[pl.BlockSpec((tm, d), lambda i, j: (i, 0)), w_cols(0), w_cols(1), w_cols(2)]
        + ride.in_specs,
        out_specs=[
            pl.BlockSpec((1, pairs_per_step, LANES_V7X, tm), feature_major_map),
            pl.BlockSpec((1, pairs_per_step, tm, 2 * LANES_V7X), row_major_map),
            pl.BlockSpec((1, heads_per_step, v_rows, tm), feature_major_map),
            pl.BlockSpec((1, tm // MOBA_BLOCK, tn),
                         lambda i, j: (i // tiles_per_seq, i % tiles_per_seq, j)),
        ] + ride.out_specs,
        out_shape=[
            jax.ShapeDtypeStruct((batch, n_pairs, LANES_V7X, seq), BF16),
            jax.ShapeDtypeStruct((batch, n_pairs, seq, 2 * LANES_V7X), BF16),
            jax.ShapeDtypeStruct((batch, N_HEADS, v_rows, seq), BF16),
            jax.ShapeDtypeStruct((batch, seq // MOBA_BLOCK, d), F32),
        ] + ride.out_shapes,
        compiler_params=_params(2),
        name="moba_qkv_proj",
    )(x, w_qkv, w_qkv, w_qkv, *ride.args)
    return (q_t, kaug, v_t, kbar), casts


def _moba_attn_kernel(q_ref, k_ref, v_ref, kbar_ref, tab_ref, o_ref, s_ref, *, head_dim):
    blk_len = MOBA_BLOCK
    tq = pair_len = 2 * blk_len
    n_tiles = q_ref.shape[3] // tq

    def slots(tile):
        return 0 if tile % 2 == 0 else n_tiles - 1

    def head_cols(h):
        return slice(h * tq, (h + 1) * tq)

    def score_unit(tile, pair, h, q_aug, cmax):
        s = jnp.dot(k_ref[0, 0, pair * pair_len:(pair + 1) * pair_len, :], q_aug[:, head_cols(h)],
                    preferred_element_type=F32)
        if pair == tile:
            s = s + tab_ref[0, blk_len:, head_cols(h)]
        elif pair == tile - 1:
            s = jnp.concatenate(
                [s[:blk_len], s[blk_len:] + tab_ref[0, :blk_len, head_cols(h)]], axis=0)
        s_ref[slots(tile) + pair, :, head_cols(h)] = s
        folded = jnp.max(s.reshape(pair_len // SUBLANES_V7X, SUBLANES_V7X, tq), axis=0)
        return folded if cmax is None else jnp.maximum(cmax, folded)

    def value_unit(tile, pair, h, col_m, acc):
        s = s_ref[slots(tile) + pair, :, head_cols(h)]
        p = jnp.exp2((s - col_m).astype(BF16))
        pv = jnp.dot(v_ref[0, h, :, pair * pair_len:(pair + 1) * pair_len], p,
                     preferred_element_type=F32)
        return pv if acc is None else acc + pv

    def finish(tile, acc):
        heads = [a[:head_dim] / a[head_dim:head_dim + 1] for a in acc]
        o_ref[0, tile * tq:(tile + 1) * tq, :] = (
            jnp.concatenate(heads, axis=0).T.astype(o_ref.dtype))

    col_m = [None, None]
    for step in range(n_tiles + 1):
        scoring, valuing = step, step - 1
        n_score = scoring + 1 if scoring < n_tiles else 0
        n_value = valuing + 1
        q_aug = _tile_queries(scoring, q_ref, kbar_ref, head_dim) if n_score else None
        cmax, acc = [None, None], [None, None]
        for pair in range(max(n_score, n_value)):
            for h in range(2):
                if pair < n_score:
                    cmax[h] = score_unit(scoring, pair, h, q_aug, cmax[h])
                if pair < n_value:
                    acc[h] = value_unit(valuing, pair, h, col_m[h], acc[h])
        if n_value:
            finish(valuing, acc)
        if n_score:
            col_m = [jnp.max(c, axis=0, keepdims=True) for c in cmax]


def _tile_queries(tile, q_ref, kbar_ref, head_dim):
    blk_len = MOBA_BLOCK
    tq = 2 * blk_len
    q_t = q_ref[0, 0, :, tile * tq:(tile + 1) * tq]
    n_blocks = kbar_ref.shape[1]
    feat_row = lax.broadcasted_iota(jnp.int32, q_t.shape, 0)
    kbar = kbar_ref[0].astype(BF16)
    blk = lax.broadcasted_iota(jnp.int32, (n_blocks, tq), 0)
    query = lax.broadcasted_iota(jnp.int32, (n_blocks, tq), 1)
    cur = 2 * tile + jnp.where(query >= blk_len, 1, 0)

    q_aug = []
    for h in range(2):
        in_head = (feat_row >= head_dim) if h else (feat_row < head_dim)
        qm = jnp.where(in_head, q_t, jnp.zeros_like(q_t))
        gate = jnp.dot(kbar, qm, preferred_element_type=F32)
        gate = jnp.where(blk < cur, gate, -jnp.inf)
        rank = jnp.zeros(gate.shape, F32)
        for m in range(min(n_blocks, 2 * tile + 1)):
            gm = gate[m:m + 1, :]
            tie = jnp.where(blk > m, 1.0, 0.0)
            rank = rank + jnp.where(gm > gate, 1.0, jnp.where(gm == gate, tie, 0.0))
        feat = jnp.where(blk < cur,
                         jnp.where(rank < MOBA_TOPK, 0.0, MASK_NEG),
                         jnp.where(blk == cur, 0.0, MASK_NEG))
        feat = jnp.concatenate(
            [feat, jnp.zeros((LANES_V7X - n_blocks, tq), F32)], axis=0)
        q_aug.append(jnp.concatenate([qm, feat.astype(BF16)], axis=0))
    return jnp.concatenate(q_aug, axis=1)


def _rel_bucket(dist):
    n = jnp.maximum(dist, 0)
    max_exact = REL_BUCKETS // 2
    nf = jnp.maximum(n, 1).astype(F32)
    large = max_exact + (jnp.log(nf / max_exact) / math.log(REL_MAX_DIST / max_exact)
                         * (REL_BUCKETS - max_exact)).astype(jnp.int32)
    large = jnp.minimum(large, REL_BUCKETS - 1)
    return jnp.where(n < max_exact, n, large)


def _bias_tables(rel_bias):
    blk = MOBA_BLOCK
    n_keys, n_q = 3 * blk, 2 * blk
    n_heads = rel_bias.shape[1]
    bias_h = rel_bias.T
    d_min = blk - (n_keys - 1)
    dist = jnp.arange(d_min, blk + n_q)
    by_dist = (bias_h[:, _rel_bucket(dist)] - bias_h[:, REL_BUCKETS - 1:]) * LOG2_E
    by_dist = jnp.where(dist[None] < 0, MASK_NEG, by_dist)
    period = by_dist.shape[1] + 1
    assert period % LANES_V7X == 0 and (period - n_q) % LANES_V7X == 0
    rows = jnp.pad(by_dist, ((0, 0), (0, 1)))[:, None, :]
    shift = (period - n_q) - (blk - d_min)

    def expand(rows_ref, out_ref):
        for h in range(2):
            wide = jnp.broadcast_to(rows_ref[h], (n_keys, period))
            rolled = pltpu.roll(wide, shift, 1, stride=1, stride_axis=0)
            out_ref[0, :, h * n_q:(h + 1) * n_q] = rolled[:, period - n_q:]

    return pl.pallas_call(
        expand,
        grid=(n_heads // 2,),
        in_specs=[pl.BlockSpec((2, 1, period), lambda p: (p, 0, 0))],
        out_specs=pl.BlockSpec((1, n_keys, 2 * n_q), lambda p: (p, 0, 0)),
        out_shape=jax.ShapeDtypeStruct((n_heads // 2, n_keys, 2 * n_q), F32),
        compiler_params=_params(1),
        name="rel_bias_tables",
    )(rows)


def _moba_attention(q_t, kaug, v_t, kbar, tables, *, batch, seq):
    n_pairs = q_t.shape[1]
    d = n_pairs * LANES_V7X
    tq = 2 * MOBA_BLOCK
    n_blocks = seq // MOBA_BLOCK
    v_rows = v_t.shape[2]
    return pl.pallas_call(
        functools.partial(_moba_attn_kernel, head_dim=d // N_HEADS),
        grid=(n_pairs, batch),
        in_specs=[
            pl.BlockSpec((1, 1, LANES_V7X, seq), lambda p, b: (b, p, 0, 0)),
            pl.BlockSpec((1, 1, seq, 2 * LANES_V7X), lambda p, b: (b, p, 0, 0)),
            pl.BlockSpec((1, 2, v_rows, seq), lambda p, b: (b, p, 0, 0)),
            pl.BlockSpec((1, n_blocks, LANES_V7X), lambda p, b: (b, 0, p)),
            pl.BlockSpec((1,) + tables.shape[1:], lambda p, b: (p, 0, 0),
                         pipeline_mode=pl.Buffered(1)),
        ],
        out_specs=pl.BlockSpec((1, seq, LANES_V7X), lambda p, b: (b, 0, p)),
        out_shape=jax.ShapeDtypeStruct((batch, seq, d), BF16),
        scratch_shapes=[
            pltpu.VMEM((2 * (seq // tq) - 1, tq, 2 * tq), F32),
        ],
        compiler_params=_params(2),
        name="moba_attention",
    )(q_t, kaug, v_t, kbar, tables)


def _tiles(seq):
    assert seq % (SUBLANES_V7X * MOBA_BLOCK) == 0, "qkv tile must hold 8 whole key blocks"
    return dict(tm_mix=min(seq, 1024), tm_ffn=min(seq, 512), tm_qkv=SUBLANES_V7X * MOBA_BLOCK,
                tn_qkv=2 * LANES_V7X, fc_ffn=2 * LANES_V7X)


def kernel(x, ln_mix_g, ln_mix_b, ln_ffn_g, ln_ffn_b, a_w_in, a_ln_g, a_ln_b, a_w_s, a_b_s,
           a_w_out, b_w_qkv, b_w_o, rel_bias, f_w_up, f_conv_w, f_conv_b, f_w_down):
    batch, seq, d = x.shape
    depth = ln_mix_g.shape[0]
    alpha = (2 * depth) ** 0.25
    cfg = _tiles(seq)
    width = a_w_out.shape[1]
    n_groups = a_w_s.shape[1]

    def rows(p):
        return p[:, None, :]

    ln_mix_g, ln_mix_b, ln_ffn_g, ln_ffn_b, a_ln_g, a_ln_b, f_conv_b = (
        rows(p) for p in (ln_mix_g, ln_mix_b, ln_ffn_g, ln_ffn_b, a_ln_g, a_ln_b, f_conv_b))
    bs_tile = jnp.repeat(a_b_s.transpose(0, 2, 1), width // n_groups, axis=2)
    tables = _bias_tables(rel_bias)

    def layer_sources(i):
        mixer = (a_w_in, i // 2) if i % 2 == 0 else (b_w_qkv, i // 2)
        proj = (a_w_out, i // 2) if i % 2 == 0 else (b_w_o, i // 2)
        return [mixer, proj, (f_w_up, i), (f_w_down, i)]

    mixer_w = a_w_in[0].astype(BF16)
    channel_w = None
    h = x.reshape(batch * seq, d)
    for i in range(depth):
        j = i // 2
        first_casts = layer_sources(0)[1:] if i == 0 else []
        if i % 2 == 0:
            mix, casts = _gmlp_mixer(h, mixer_w, a_ln_g, a_ln_b, a_w_s, bs_tile, j, first_casts,
                                     tm=cfg["tm_mix"])
        else:
            qkv, casts = _qkv_proj(h, mixer_w, first_casts, batch=batch, seq=seq,
                                   tm=cfg["tm_qkv"], tn=cfg["tn_qkv"])
            mix = _moba_attention(*qkv, tables, batch=batch, seq=seq).reshape(batch * seq, d)
        w_proj, w_up, w_down = casts or channel_w
        next_sources = layer_sources(i + 1) if i + 1 < depth else []
        h, next_w = _channel_stage(mix, h, w_proj, ln_mix_g, ln_mix_b, w_up, f_conv_w, f_conv_b,
                                   w_down, ln_ffn_g, ln_ffn_b, i, next_sources, alpha=alpha,
                                   tm=cfg["tm_ffn"], fc=cfg["fc_ffn"], seq=seq)
        if next_w:
            mixer_w, *channel_w = next_w
    return h.reshape(batch, seq, d)
```
